```python
import math
import jax, jax.numpy as jnp
from jax import lax
import numpy as np

D_MODEL = 2048
BATCH = 2
SEQ = 4096
DEPTH = 1
DEC_BATCH = 32
DEC_SEQ = 1
PAST_LEN = 8192
PAGE_SIZE = 128

MIX_WIDTH = D_MODEL
HEAD_DIM = 128
SB_HEADS = (MIX_WIDTH // 2) // HEAD_DIM
CM_DIM = 128
CM_HEADS = (MIX_WIDTH - SB_HEADS * HEAD_DIM) // CM_DIM
SB_W = SB_HEADS * HEAD_DIM
CM_W = CM_HEADS * CM_DIM
IN_W = 3 * SB_W + 2 * CM_W
CHUNK = 128
Q_BLOCK = 128
D_FF = ((8 * D_MODEL // 3 + 255) // 256) * 256
PLE_DIM = 256
EPS = 1e-6
SB_SCALE = 1.0 / math.sqrt(HEAD_DIM)
SB_BIAS_MIN = 2.0
SB_BIAS_MAX = 8.0

kernel_name = "hymba_stickbreak_chunkmlp_decode_step"


def rmsnorm(x, g):
    xf = x.astype(jnp.float32)
    y = xf * lax.rsqrt(jnp.mean(xf * xf, axis=-1, keepdims=True) + EPS)
    return (y * g.astype(jnp.float32)).astype(x.dtype)


def layernorm(x, g, b):
    xf = x.astype(jnp.float32)
    mu = jnp.mean(xf, axis=-1, keepdims=True)
    xc = xf - mu
    y = xc * lax.rsqrt(jnp.mean(xc * xc, axis=-1, keepdims=True) + EPS)
    return (y * g.astype(jnp.float32) + b.astype(jnp.float32)).astype(x.dtype)


def mixer_inputs(h, g_mix, w_in, g_q, g_k, sgu_ln_g, sgu_ln_b):
    B, L, _ = h.shape
    z = rmsnorm(h, g_mix) @ w_in
    q, k, v, u, vc = jnp.split(z, [SB_W, 2 * SB_W, 3 * SB_W, 3 * SB_W + CM_W], axis=-1)
    q = rmsnorm(q.reshape(B, L, SB_HEADS, HEAD_DIM), g_q)
    k = rmsnorm(k.reshape(B, L, SB_HEADS, HEAD_DIM), g_k)
    v = v.reshape(B, L, SB_HEADS, HEAD_DIM)
    u = jax.nn.gelu(u, approximate=False).reshape(B, L, CM_HEADS, CM_DIM)
    vc = jax.nn.gelu(vc, approximate=False).reshape(B, L, CM_HEADS, CM_DIM)
    vc = layernorm(vc, sgu_ln_g, sgu_ln_b)
    return q, k, v, u, vc


def stick_breaking(q, k, v, sb_bias, q_pos, k_pos):
    z = jnp.einsum('bqhd,bkhd->bhqk', q.astype(jnp.float32), k.astype(jnp.float32)) * SB_SCALE
    z = z + sb_bias.astype(jnp.float32)[None, :, None, None]
    causal = k_pos[None, :] < q_pos[:, None]
    log_keep = jnp.where(causal, jax.nn.log_sigmoid(-z), 0.0)
    after = lax.cumsum(log_keep, axis=3, reverse=True) - log_keep
    a = jnp.where(causal, jnp.exp(jax.nn.log_sigmoid(z) + after), 0.0)
    return jnp.einsum('bhqk,bkhd->bqhd', a, v.astype(jnp.float32)).astype(v.dtype)


def stick_breaking_prompt(q, k, v, sb_bias):
    B, L, H, D = q.shape
    nb = L // Q_BLOCK
    qb = q.reshape(B, nb, Q_BLOCK, H, D).transpose(1, 0, 2, 3, 4)
    pos = jnp.arange(L, dtype=jnp.int32)
    qpos_b = pos.reshape(nb, Q_BLOCK)
    out = lax.map(lambda args: stick_breaking(args[0], k, v, sb_bias, args[1], pos), (qb, qpos_b))
    return out.transpose(1, 0, 2, 3, 4).reshape(B, L, H, D)


def chunk_mlp(u, vn, w_s, b_s):
    B, L, H, D = vn.shape
    pad = (-L) % CHUNK
    vp = jnp.pad(vn, ((0, 0), (0, pad), (0, 0), (0, 0)))
    nc = (L + pad) // CHUNK
    vch = vp.reshape(B, nc, CHUNK, H, D)
    mask = jnp.tril(jnp.ones((CHUNK, CHUNK), dtype=bool))
    wm = jnp.where(mask[None], w_s, 0.0)
    mixed = jnp.einsum('hts,bcshd->bcthd', wm, vch) + b_s.T[:, :, None]
    mixed = mixed.reshape(B, nc * CHUNK, H, D)[:, :L]
    return (u * mixed).astype(u.dtype)


def layer_out(h, o_sb, o_cm, p, g_out_sb, g_out_cm, w_out, g_ffn, w_gate, w_up, w_down,
              g_ple, w_ple_gate, w_ple_proj):
    B, L, _ = h.shape
    o = jnp.concatenate([rmsnorm(o_sb.reshape(B, L, SB_W), g_out_sb),
                         rmsnorm(o_cm.reshape(B, L, CM_W), g_out_cm)], axis=-1)
    h = h + o @ w_out
    f = rmsnorm(h, g_ffn)
    h = h + (jax.nn.silu(f @ w_gate) * (f @ w_up)) @ w_down
    h = h + jax.nn.sigmoid(rmsnorm(h, g_ple) @ w_ple_gate) * (p @ w_ple_proj)
    return h


def setup_inputs(seed: int = 0) -> dict:
    key = jax.random.key(seed)
    ks = jax.random.split(key, 32)
    n_pages = PAST_LEN // PAGE_SIZE
    n_used = DEC_BATCH * n_pages
    n_phys = n_used + max(1, n_used // 4)

    def nrm(k, shape, s):
        return jax.random.normal(k, shape, jnp.float32) * s

    def gain(k, shape):
        return 1.0 + 0.1 * jax.random.normal(k, shape, jnp.float32)

    page_table = jax.random.permutation(ks[4], n_phys)[:n_used].reshape(DEC_BATCH, n_pages).astype(jnp.int32)
    sb_bias = (-jnp.linspace(SB_BIAS_MIN, SB_BIAS_MAX, SB_HEADS, dtype=jnp.float32)[None, :]
               + nrm(ks[25], (DEPTH, SB_HEADS), 0.1))
    return {
        "x_prompt": nrm(ks[0], (BATCH, SEQ, D_MODEL), 1.0),
        "x_sample": nrm(ks[1], (DEC_BATCH, DEC_SEQ, D_MODEL), 1.0),
        "cache_k": nrm(ks[2], (DEPTH, n_phys, PAGE_SIZE, SB_HEADS, HEAD_DIM), 1.0),
        "cache_v": nrm(ks[3], (DEPTH, n_phys, PAGE_SIZE, SB_HEADS, HEAD_DIM), 1.0),
        "page_table": page_table,
        "p_prompt": nrm(ks[5], (DEPTH, BATCH, SEQ, PLE_DIM), 1.0),
        "p_sample": nrm(ks[6], (DEPTH, DEC_BATCH, DEC_SEQ, PLE_DIM), 1.0),
        "g_mix": gain(ks[7], (DEPTH, D_MODEL)),
        "w_in": nrm(ks[8], (DEPTH, D_MODEL, IN_W), D_MODEL ** -0.5),
        "g_q": gain(ks[9], (DEPTH, HEAD_DIM)),
        "g_k": gain(ks[10], (DEPTH, HEAD_DIM)),
        "sb_bias": sb_bias,
        "sgu_ln_g": gain(ks[11], (DEPTH, CM_HEADS, CM_DIM)),
        "sgu_ln_b": nrm(ks[12], (DEPTH, CM_HEADS, CM_DIM), 0.02),
        "w_s": nrm(ks[13], (DEPTH, CM_HEADS, CHUNK, CHUNK), CHUNK ** -0.5),
        "b_s": gain(ks[14], (DEPTH, CM_HEADS, CHUNK)),
        "g_out_sb": gain(ks[15], (DEPTH, SB_W)),
        "g_out_cm": gain(ks[16], (DEPTH, CM_W)),
        "w_out": nrm(ks[17], (DEPTH, MIX_WIDTH, D_MODEL), MIX_WIDTH ** -0.5),
        "g_ffn": gain(ks[18], (DEPTH, D_MODEL)),
        "w_gate": nrm(ks[19], (DEPTH, D_MODEL, D_FF), D_MODEL ** -0.5),
        "w_up": nrm(ks[20], (DEPTH, D_MODEL, D_FF), D_MODEL ** -0.5),
        "w_down": nrm(ks[21], (DEPTH, D_FF, D_MODEL), D_FF ** -0.5),
        "g_ple": gain(ks[22], (DEPTH, D_MODEL)),
        "w_ple_gate": nrm(ks[23], (DEPTH, D_MODEL, D_MODEL), D_MODEL ** -0.5),
        "w_ple_proj": nrm(ks[24], (DEPTH, PLE_DIM, D_MODEL), PLE_DIM ** -0.5),
    }


def reference(x_prompt, x_sample, cache_k, cache_v, page_table, p_prompt, p_sample,
              g_mix, w_in, g_q, g_k, sb_bias, sgu_ln_g, sgu_ln_b, w_s, b_s, g_out_sb, g_out_cm,
              w_out, g_ffn, w_gate, w_up, w_down, g_ple, w_ple_gate, w_ple_proj):
    dec_b, dec_len = x_sample.shape[0], x_sample.shape[1]
    past_len = page_table.shape[1] * cache_k.shape[2]
    q_pos_s = past_len + jnp.arange(dec_len, dtype=jnp.int32)
    k_pos_s = jnp.arange(past_len + dec_len, dtype=jnp.int32)
    hp, hs = x_prompt, x_sample
    kp_l, vp_l, ks_l, vs_l, cs_l = [], [], [], [], []
    for i in range(DEPTH):
        qp, kp, vp, up, cp = mixer_inputs(hp, g_mix[i], w_in[i], g_q[i], g_k[i], sgu_ln_g[i], sgu_ln_b[i])
        o_sb_p = stick_breaking_prompt(qp, kp, vp, sb_bias[i])
        o_cm_p = chunk_mlp(up, cp, w_s[i], b_s[i])
        hp = layer_out(hp, o_sb_p, o_cm_p, p_prompt[i], g_out_sb[i], g_out_cm[i], w_out[i],
                       g_ffn[i], w_gate[i], w_up[i], w_down[i], g_ple[i], w_ple_gate[i], w_ple_proj[i])
        qs, kn, vn, us, cs = mixer_inputs(hs, g_mix[i], w_in[i], g_q[i], g_k[i], sgu_ln_g[i], sgu_ln_b[i])
        k_past = cache_k[i][page_table].reshape(dec_b, past_len, SB_HEADS, HEAD_DIM)
        v_past = cache_v[i][page_table].reshape(dec_b, past_len, SB_HEADS, HEAD_DIM)
        k_all = jnp.concatenate([k_past.astype(kn.dtype), kn], axis=1)
        v_all = jnp.concatenate([v_past.astype(vn.dtype), vn], axis=1)
        o_sb_s = stick_breaking(qs, k_all, v_all, sb_bias[i], q_pos_s, k_pos_s)
        o_cm_s = chunk_mlp(us, cs, w_s[i], b_s[i])
        hs = layer_out(hs, o_sb_s, o_cm_s, p_sample[i], g_out_sb[i], g_out_cm[i], w_out[i],
                       g_ffn[i], w_gate[i], w_up[i], w_down[i], g_ple[i], w_ple_gate[i], w_ple_proj[i])
        kp_l.append(kp)
        vp_l.append(vp)
        ks_l.append(kn)
        vs_l.append(vn)
        cs_l.append(cs)
    k_prompt_new = jnp.stack(kp_l)
    v_prompt_new = jnp.stack(vp_l)
    k_sample_new = jnp.stack(ks_l)
    v_sample_new = jnp.stack(vs_l)
    chunk_v_sample = jnp.stack(cs_l)
    return (hp, hs, k_prompt_new, v_prompt_new, k_sample_new, v_sample_new, chunk_v_sample)
```

```python
import functools
import math

import jax
import jax.numpy as jnp
from jax import lax
from jax.experimental import pallas as pl
from jax.experimental.pallas import tpu as pltpu

HEAD_DIM = 128
CHUNK = 128
EPS = 1e-6
SB_SCALE = 1.0 / math.sqrt(HEAD_DIM)
INV_SQRT2 = 0.7071067811865476

LANES = 128
VMEM_LIMIT = 56 * 1024 * 1024

F32 = jnp.float32
BF16 = jnp.bfloat16


def _cparams(*sem):
    return pltpu.CompilerParams(dimension_semantics=sem, vmem_limit_bytes=VMEM_LIMIT)


def _rms_rows(x):
    return x * lax.rsqrt(jnp.mean(x * x, axis=-1, keepdims=True) + EPS)


def _gelu(x):
    return 0.5 * x * (1.0 + lax.erf(x * INV_SQRT2))


def _log_sigmoid_pair(z):
    ls = jnp.minimum(z, 0.0) - jnp.log(1.0 + jnp.exp(-jnp.abs(z)))
    return ls, ls - z


def _split_bf16(x):
    hi = x.astype(BF16)
    lo = (x - hi.astype(F32)).astype(BF16)
    return hi, lo


def _inproj_kernel(x_ref, gmix_ref, w_ref, gq_ref, gk_ref, lng_ref, lnb_ref, ws_ref, bs_ref, gcm_ref,
                   q_ref, kf_ref, kb_ref, vf_ref, vb_ref, ocm_ref, *rest, n_heads, fresh_chunk_rows):
    cs_ref = rest[0] if fresh_chunk_rows else None
    xn_ref, u_ref = rest[-2:]
    j = pl.program_id(1)

    @pl.when(j == 0)
    def _():
        xn_ref[...] = (_rms_rows(x_ref[...]) * gmix_ref[...]).astype(BF16)

    z = jnp.dot(xn_ref[...], w_ref[...], preferred_element_type=F32)
    tm = z.shape[0]

    def head(a, h):
        return a[:, h * HEAD_DIM:(h + 1) * HEAD_DIM]

    @pl.when(j == 0)
    def _():
        for h in range(n_heads):
            q = _rms_rows(head(z, h)) * gq_ref[...]
            q_ref[:, h * HEAD_DIM:(h + 1) * HEAD_DIM] = (q * SB_SCALE).astype(q_ref.dtype)

    @pl.when(j == 1)
    def _():
        for h in range(n_heads):
            k = _rms_rows(head(z, h)) * gk_ref[...]
            kf_ref[:, h * HEAD_DIM:(h + 1) * HEAD_DIM] = k
            kb_ref[:, h * HEAD_DIM:(h + 1) * HEAD_DIM] = k.astype(BF16)

    @pl.when(j == 2)
    def _():
        vf_ref[...] = z
        vb_ref[...] = z.astype(BF16)

    @pl.when(j == 3)
    def _():
        u_ref[...] = _gelu(z)

    @pl.when(j == 4)
    def _():
        g = _gelu(z)
        if not fresh_chunk_rows:
            row = lax.broadcasted_iota(jnp.int32, (CHUNK, CHUNK), 0)
            col = lax.broadcasted_iota(jnp.int32, (CHUNK, CHUNK), 1)
            tril = col <= row
        for h in range(n_heads):
            gh = head(g, h)
            mu = jnp.mean(gh, axis=-1, keepdims=True)
            xc = gh - mu
            vc = xc * lax.rsqrt(jnp.mean(xc * xc, axis=-1, keepdims=True) + EPS)
            vc = vc * lng_ref[:, h * HEAD_DIM:(h + 1) * HEAD_DIM] + lnb_ref[:, h * HEAD_DIM:(h + 1) * HEAD_DIM]
            if fresh_chunk_rows:
                cs_ref[:, h * HEAD_DIM:(h + 1) * HEAD_DIM] = vc
                mixed = ws_ref[h, 0:1, 0:1] * vc + bs_ref[0:1, h:h + 1]
                u_ref[:, h * HEAD_DIM:(h + 1) * HEAD_DIM] = u_ref[:, h * HEAD_DIM:(h + 1) * HEAD_DIM] * mixed
            else:
                wm = jnp.where(tril, ws_ref[h], 0.0).astype(BF16)
                for c in range(tm // CHUNK):
                    rows = slice(c * CHUNK, (c + 1) * CHUNK)
                    mixed = jnp.dot(wm, vc[rows].astype(BF16), preferred_element_type=F32)
                    mixed = mixed + bs_ref[:, h:h + 1]
                    u_ref[rows, h * HEAD_DIM:(h + 1) * HEAD_DIM] = (
                        u_ref[rows, h * HEAD_DIM:(h + 1) * HEAD_DIM] * mixed)
        ocm_ref[...] = (_rms_rows(u_ref[...]) * gcm_ref[...]).astype(BF16)


def _in_proj(x, g_mix, w_in_b, g_q, g_k, ln_g, ln_b, w_s, b_s_t, g_out_cm, *, tm, fresh_chunk_rows):
    T, d_model = x.shape
    n_heads = w_s.shape[0]
    gw = n_heads * HEAD_DIM
    assert w_in_b.shape == (d_model, 5 * gw) and T % tm == 0
    assert fresh_chunk_rows or tm % CHUNK == 0
    row_blk = lambda i, j: (i, 0)
    const2 = lambda i, j: (0, 0)
    out_f32 = jax.ShapeDtypeStruct((T, gw), F32)
    out_b16 = jax.ShapeDtypeStruct((T, gw), BF16)
    rows = lambda: pl.BlockSpec((tm, gw), row_blk)
    out_shape = [out_f32 if fresh_chunk_rows else out_b16, out_f32, out_b16, out_f32, out_b16, out_b16]
    if fresh_chunk_rows:
        out_shape.append(out_f32)
    return pl.pallas_call(
        functools.partial(_inproj_kernel, n_heads=n_heads, fresh_chunk_rows=fresh_chunk_rows),
        grid=(T // tm, 5),
        in_specs=[
            pl.BlockSpec((tm, d_model), row_blk),
            pl.BlockSpec((1, d_model), const2),
            pl.BlockSpec((d_model, gw), lambda i, j: (0, j)),
            pl.BlockSpec((1, HEAD_DIM), const2),
            pl.BlockSpec((1, HEAD_DIM), const2),
            pl.BlockSpec((1, gw), const2),
            pl.BlockSpec((1, gw), const2),
            pl.BlockSpec((n_heads, CHUNK, CHUNK), lambda i, j: (0, 0, 0)),
            pl.BlockSpec((CHUNK, n_heads), const2),
            pl.BlockSpec((1, gw), const2),
        ],
        out_specs=[rows() for _ in out_shape],
        out_shape=out_shape,
        scratch_shapes=[pltpu.VMEM((tm, d_model), BF16), pltpu.VMEM((tm, gw), F32)],
        compiler_params=_cparams("arbitrary", "arbitrary"),
        name="in_proj",
    )(x, g_mix, w_in_b, g_q, g_k, ln_g, ln_b, w_s, b_s_t, g_out_cm)


def _sb_prompt_kernel(bias_ref, q_ref, k_ref, v_ref, o_ref, *, blk):
    h = pl.program_id(1)
    i = pl.program_id(2)
    bias = bias_ref[h]
    q = q_ref[0]
    row = lax.broadcasted_iota(jnp.int32, (blk, blk), 0)
    col = lax.broadcasted_iota(jnp.int32, (blk, blk), 1)
    later = jnp.where(row > col, 1.0, 0.0).astype(BF16)
    causal = col < row

    def block(j, run, acc, on_diagonal):
        start = pl.multiple_of(j * blk, blk)
        kb = k_ref[0, pl.ds(start, blk), :]
        vb = v_ref[0, pl.ds(start, blk), :]
        z = lax.dot_general(q, kb, (((1,), (1,)), ((), ())), preferred_element_type=F32) + bias
        ls, lk = _log_sigmoid_pair(z)
        if on_diagonal:
            lk = jnp.where(causal, lk, 0.0)
        hi, lo = _split_bf16(lk)
        after = (jnp.dot(hi, later, preferred_element_type=F32)
                 + jnp.dot(lo, later, preferred_element_type=F32))
        a = jnp.exp(ls + after + run)
        if on_diagonal:
            a = jnp.where(causal, a, 0.0)
        acc = acc + jnp.dot(a.astype(BF16), vb, preferred_element_type=F32)
        run = run + jnp.sum(lk, axis=-1, keepdims=True)
        return run, acc

    run, acc = block(i, jnp.zeros((blk, 1), F32), jnp.zeros((blk, HEAD_DIM), F32), True)

    def body(t, carry):
        return block(i - 1 - t, carry[0], carry[1], False)

    run, acc = lax.fori_loop(0, i, body, (run, acc))
    o_ref[0] = acc


def _sb_prompt(q_b, k_b, v_b, sb_bias, *, blk):
    B, L, sbw = q_b.shape
    n_heads = sbw // HEAD_DIM
    assert L % blk == 0
    qo_spec = lambda: pl.BlockSpec((1, blk, HEAD_DIM), lambda b, h, i, bias: (b, i, h))
    kv_spec = lambda: pl.BlockSpec((1, L, HEAD_DIM), lambda b, h, i, bias: (b, 0, h))
    return pl.pallas_call(
        functools.partial(_sb_prompt_kernel, blk=blk),
        grid_spec=pltpu.PrefetchScalarGridSpec(
            num_scalar_prefetch=1,
            grid=(B, n_heads, L // blk),
            in_specs=[qo_spec(), kv_spec(), kv_spec()],
            out_specs=qo_spec(),
        ),
        out_shape=jax.ShapeDtypeStruct((B, L, sbw), F32),
        compiler_params=_cparams("arbitrary", "arbitrary", "arbitrary"),
        name="sb_prompt",
    )(sb_bias, q_b, k_b, v_b)


PAGES_PER_STEP = 8


def _sb_decode_kernel(pt_ref, q_ref, bias_ref, later_ref, expand_ref, *refs, n_heads):
    k_refs = refs[:PAGES_PER_STEP]
    v_refs = refs[PAGES_PER_STEP:2 * PAGES_PER_STEP]
    o_ref, qrows_ref, run_ref, acc_ref = refs[2 * PAGES_PER_STEP:]
    c = pl.program_id(1)
    page, sbw = k_refs[0].shape[1:]

    @pl.when(c == 0)
    def _():
        r = lax.broadcasted_iota(jnp.int32, (LANES, sbw), 0)
        l = lax.broadcasted_iota(jnp.int32, (LANES, sbw), 1)
        qb = jnp.broadcast_to(q_ref[0], (LANES, sbw))
        qrows_ref[...] = jnp.where(l // HEAD_DIM == r, qb, 0.0).astype(BF16)
        run_ref[...] = jnp.zeros_like(run_ref)
        acc_ref[...] = jnp.zeros_like(acc_ref)

    later = later_ref[...]
    expand = expand_ref[...]
    for r in reversed(range(PAGES_PER_STEP)):
        kp = k_refs[r][0].astype(BF16)
        z = lax.dot_general(kp, qrows_ref[...], (((1,), (1,)), ((), ())),
                            preferred_element_type=F32) + bias_ref[...]
        ls, lk = _log_sigmoid_pair(z)
        hi, lo = _split_bf16(lk)
        after = (jnp.dot(later, hi, preferred_element_type=F32)
                 + jnp.dot(later, lo, preferred_element_type=F32)) + run_ref[...]
        a = jnp.exp(ls + after)
        run_ref[...] = after[0:1, :] + lk[0:1, :]
        ahi, alo = _split_bf16(a)
        a_wide = (jnp.dot(ahi, expand, preferred_element_type=F32)
                  + jnp.dot(alo, expand, preferred_element_type=F32))
        prod = a_wide * v_refs[r][0]
        acc_ref[...] += prod.reshape(page // 8, 8, sbw).sum(axis=0)

    @pl.when(c == pl.num_programs(1) - 1)
    def _():
        o_ref[0] = jnp.sum(acc_ref[...], axis=0, keepdims=True)


def _sb_decode(q, cache_k, cache_v, page_table, sb_bias_row):
    B, sbw = q.shape
    n_phys, page, n_heads, _ = cache_k.shape
    n_pages = page_table.shape[1]
    assert n_pages % PAGES_PER_STEP == 0 and n_heads <= LANES and page % 8 == 0
    n_steps = n_pages // PAGES_PER_STEP
    ck = cache_k.reshape(n_phys, page, sbw)
    cv = cache_v.reshape(n_phys, page, sbw)
    s = jnp.arange(page)
    later = (s[None, :] > s[:, None]).astype(BF16)
    lane = jnp.arange(sbw)
    expand = (lane[None, :] // HEAD_DIM == jnp.arange(LANES)[:, None]).astype(BF16)

    def page_spec(r):
        return pl.BlockSpec(
            (1, page, sbw),
            lambda b, c, pt: (pt[b, (n_steps - 1 - c) * PAGES_PER_STEP + r], 0, 0))

    const2 = lambda b, c, pt: (0, 0)
    out = pl.pallas_call(
        functools.partial(_sb_decode_kernel, n_heads=n_heads),
        grid_spec=pltpu.PrefetchScalarGridSpec(
            num_scalar_prefetch=1,
            grid=(B, n_steps),
            in_specs=[pl.BlockSpec((1, 1, sbw), lambda b, c, pt: (b, 0, 0)),
                      pl.BlockSpec((1, LANES), const2),
                      pl.BlockSpec((page, page), const2),
                      pl.BlockSpec((LANES, sbw), const2)]
                     + [page_spec(r) for r in range(PAGES_PER_STEP)] * 2,
            out_specs=pl.BlockSpec((1, 1, sbw), lambda b, c, pt: (b, 0, 0)),
            scratch_shapes=[pltpu.VMEM((LANES, sbw), BF16),
                            pltpu.VMEM((1, LANES), F32),
                            pltpu.VMEM((8, sbw), F32)],
        ),
        out_shape=jax.ShapeDtypeStruct((B, 1, sbw), F32),
        compiler_params=_cparams("arbitrary", "arbitrary"),
        name="sb_decode",
    )(page_table, q.reshape(B, 1, sbw), sb_bias_row, later, expand,
      *([ck] * PAGES_PER_STEP), *([cv] * PAGES_PER_STEP))
    return out.reshape(B, sbw)


def _out_proj_kernel(osb_ref, ocm_ref, h_ref, gsb_ref, w_ref, gffn_ref, h1_ref, f_ref, o_scr):
    sbw = osb_ref.shape[1]
    o_scr[:, :sbw] = (_rms_rows(osb_ref[...]) * gsb_ref[...]).astype(BF16)
    o_scr[:, sbw:] = ocm_ref[...]
    h1 = h_ref[...] + jnp.dot(o_scr[...], w_ref[...], preferred_element_type=F32)
    h1_ref[...] = h1
    f_ref[...] = (_rms_rows(h1) * gffn_ref[...]).astype(BF16)


def _out_proj(o_sb, o_cm_n, h, g_out_sb, w_out_b, g_ffn, *, tm):
    T, d_model = h.shape
    sbw, cmw = o_sb.shape[1], o_cm_n.shape[1]
    assert T % tm == 0 and w_out_b.shape == (sbw + cmw, d_model)
    row_blk = lambda i: (i, 0)
    const2 = lambda i: (0, 0)
    return pl.pallas_call(
        _out_proj_kernel,
        grid=(T // tm,),
        in_specs=[pl.BlockSpec((tm, sbw), row_blk),
                  pl.BlockSpec((tm, cmw), row_blk),
                  pl.BlockSpec((tm, d_model), row_blk),
                  pl.BlockSpec((1, sbw), const2),
                  pl.BlockSpec((sbw + cmw, d_model), const2),
                  pl.BlockSpec((1, d_model), const2)],
        out_specs=[pl.BlockSpec((tm, d_model), row_blk), pl.BlockSpec((tm, d_model), row_blk)],
        out_shape=[jax.ShapeDtypeStruct((T, d_model), F32), jax.ShapeDtypeStruct((T, d_model), BF16)],
        scratch_shapes=[pltpu.VMEM((tm, sbw + cmw), BF16)],
        compiler_params=_cparams("arbitrary"),
        name="out_proj",
    )(o_sb, o_cm_n, h, g_out_sb, w_out_b, g_ffn)


def _ffn_kernel(f_ref, h1_ref, wg_ref, wu_ref, wd_ref, gple_ref, h2_ref, hn_ref, acc_ref):
    j = pl.program_id(1)

    @pl.when(j == 0)
    def _():
        acc_ref[...] = h1_ref[...]

    f = f_ref[...]
    gate = jnp.dot(f, wg_ref[...], preferred_element_type=F32)
    up = jnp.dot(f, wu_ref[...], preferred_element_type=F32)
    act = (gate * jax.nn.sigmoid(gate) * up).astype(BF16)
    acc_ref[...] += jnp.dot(act, wd_ref[...], preferred_element_type=F32)

    @pl.when(j == pl.num_programs(1) - 1)
    def _():
        h2 = acc_ref[...]
        h2_ref[...] = h2
        hn_ref[...] = (_rms_rows(h2) * gple_ref[...]).astype(BF16)


def _ffn(f, h1, wg_b, wu_b, wd_b, g_ple, *, tm, tf):
    T, d_model = h1.shape
    d_ff = wg_b.shape[1]
    assert T % tm == 0 and d_ff % tf == 0
    row_blk = lambda i, j: (i, 0)
    return pl.pallas_call(
        _ffn_kernel,
        grid=(T // tm, d_ff // tf),
        in_specs=[pl.BlockSpec((tm, d_model), row_blk),
                  pl.BlockSpec((tm, d_model), row_blk),
                  pl.BlockSpec((d_model, tf), lambda i, j: (0, j)),
                  pl.BlockSpec((d_model, tf), lambda i, j: (0, j)),
                  pl.BlockSpec((tf, d_model), lambda i, j: (j, 0)),
                  pl.BlockSpec((1, d_model), lambda i, j: (0, 0))],
        out_specs=[pl.BlockSpec((tm, d_model), row_blk), pl.BlockSpec((tm, d_model), row_blk)],
        out_shape=[jax.ShapeDtypeStruct((T, d_model), F32), jax.ShapeDtypeStruct((T, d_model), BF16)],
        scratch_shapes=[pltpu.VMEM((tm, d_model), F32)],
        compiler_params=_cparams("arbitrary", "arbitrary"),
        name="ffn",
    )(f, h1, wg_b, wu_b, wd_b, g_ple)


def _ple_kernel(hn_ref, h2_ref, p_ref, wg_ref, wp_ref, o_ref):
    gate = jax.nn.sigmoid(jnp.dot(hn_ref[...], wg_ref[...], preferred_element_type=F32))
    proj = jnp.dot(p_ref[...].astype(BF16), wp_ref[...], preferred_element_type=F32)
    o_ref[...] = h2_ref[...] + gate * proj


def _ple(hn, h2, p, wpg_b, wpp_b, *, tm):
    T, d_model = h2.shape
    ple_dim = p.shape[1]
    assert T % tm == 0
    row_blk = lambda i: (i, 0)
    const2 = lambda i: (0, 0)
    return pl.pallas_call(
        _ple_kernel,
        grid=(T // tm,),
        in_specs=[pl.BlockSpec((tm, d_model), row_blk),
                  pl.BlockSpec((tm, d_model), row_blk),
                  pl.BlockSpec((tm, ple_dim), row_blk),
                  pl.BlockSpec((d_model, d_model), const2),
                  pl.BlockSpec((ple_dim, d_model), const2)],
        out_specs=pl.BlockSpec((tm, d_model), row_blk),
        out_shape=jax.ShapeDtypeStruct((T, d_model), F32),
        compiler_params=_cparams("arbitrary"),
        name="ple",
    )(hn, h2, p, wpg_b, wpp_b)


def _layer_tail(h, o_sb, o_cm_n, p, g_out_sb, w_out_b, g_ffn, wg_b, wu_b, wd_b, g_ple, wpg_b, wpp_b,
                *, tm_proj, tm_ffn, tf):
    h1, f = _out_proj(o_sb, o_cm_n, h, g_out_sb, w_out_b, g_ffn, tm=tm_proj)
    h2, hn = _ffn(f, h1, wg_b, wu_b, wd_b, g_ple, tm=tm_ffn, tf=tf)
    return _ple(hn, h2, p, wpg_b, wpp_b, tm=tm_proj)


def kernel(x_prompt, x_sample, cache_k, cache_v, page_table, p_prompt, p_sample, g_mix, w_in, g_q, g_k, sb_bias, sgu_ln_g, sgu_ln_b, w_s, b_s, g_out_sb, g_out_cm, w_out, g_ffn, w_gate, w_up, w_down, g_ple, w_ple_gate, w_ple_proj):
    depth = w_in.shape[0]
    B, L, d_model = x_prompt.shape
    Bs, Ls, _ = x_sample.shape
    n_sb = cache_k.shape[3]
    n_cm = w_s.shape[1]
    sbw, cmw = n_sb * HEAD_DIM, n_cm * HEAD_DIM
    assert Ls == 1 and n_sb == n_cm and cache_k.shape[4] == HEAD_DIM
    assert (page_table.shape[1] * cache_k.shape[2]) % CHUNK == 0 and L % CHUNK == 0

    hp = x_prompt.reshape(B * L, d_model)
    hs = x_sample.reshape(Bs * Ls, d_model)
    row = lambda a: a.reshape(1, -1)
    kp_l, vp_l, ks_l, vs_l, cs_l = [], [], [], [], []
    for i in range(depth):
        w_in_b = w_in[i].astype(BF16)
        w_out_b = w_out[i].astype(BF16)
        wg_b, wu_b, wd_b = w_gate[i].astype(BF16), w_up[i].astype(BF16), w_down[i].astype(BF16)
        wpg_b, wpp_b = w_ple_gate[i].astype(BF16), w_ple_proj[i].astype(BF16)
        bias_row = jnp.zeros((1, LANES), F32).at[0, :n_sb].set(sb_bias[i])
        proj_args = (row(g_mix[i]), w_in_b, row(g_q[i]), row(g_k[i]), row(sgu_ln_g[i]), row(sgu_ln_b[i]),
                     w_s[i], b_s[i].T, row(g_out_cm[i]))
        tail_args = (row(g_out_sb[i]), w_out_b, row(g_ffn[i]), wg_b, wu_b, wd_b, row(g_ple[i]), wpg_b, wpp_b)

        q_b, k_f, k_b, v_f, v_b, ocm_n = _in_proj(hp, *proj_args, tm=512, fresh_chunk_rows=False)
        o_sb = _sb_prompt(q_b.reshape(B, L, sbw), k_b.reshape(B, L, sbw), v_b.reshape(B, L, sbw),
                          sb_bias[i], blk=256)
        hp = _layer_tail(hp, o_sb.reshape(B * L, sbw), ocm_n, p_prompt[i].reshape(B * L, -1), *tail_args,
                         tm_proj=256, tm_ffn=512, tf=512)
        kp_l.append(k_f.reshape(B, L, n_sb, HEAD_DIM))
        vp_l.append(v_f.reshape(B, L, n_sb, HEAD_DIM))

        q_s, ks_f, _, vs_f, _, ocm_s, cs = _in_proj(hs, *proj_args, tm=Bs, fresh_chunk_rows=True)
        o_sb_s = _sb_decode(q_s, cache_k[i], cache_v[i], page_table, bias_row)
        hs = _layer_tail(hs, o_sb_s, ocm_s, p_sample[i].reshape(Bs, -1), *tail_args,
                         tm_proj=Bs, tm_ffn=Bs, tf=512)
        ks_l.append(ks_f.reshape(Bs, Ls, n_sb, HEAD_DIM))
        vs_l.append(vs_f.reshape(Bs, Ls, n_sb, HEAD_DIM))
        cs_l.append(cs.reshape(Bs, Ls, n_cm, HEAD_DIM))

    return (hp.reshape(B, L, d_model), hs.reshape(Bs, Ls, d_model),
            jnp.stack(kp_l), jnp.stack(vp_l), jnp.stack(ks_l), jnp.stack(vs_l), jnp.stack(cs_l))
```

```python
import functools
import math

import jax
import jax.numpy as jnp
from jax import lax
from jax.experimental import pallas as pl
from jax.experimental.pallas import tpu as pltpu

HEAD_DIM = 128
CHUNK = 128
EPS = 1e-6
SB_SCALE = 1.0 / math.sqrt(HEAD_DIM)
INV_SQRT2 = 0.7071067811865476
LOG2E = 1.4426950408889634
Q_SCALE = SB_SCALE * LOG2E

LANES = 128
VMEM_LIMIT = 56 * 1024 * 1024

F32 = jnp.float32
BF16 = jnp.bfloat16


def _cparams(*sem):
    return pltpu.CompilerParams(dimension_semantics=sem, vmem_limit_bytes=VMEM_LIMIT)


def _rms_rows(x):
    return x * lax.rsqrt(jnp.mean(x * x, axis=-1, keepdims=True) + EPS)


def _gelu(x):
    return 0.5 * x * (1.0 + lax.erf(x * INV_SQRT2))


def _log2_sigmoid_pair(z2):
    ls = jnp.minimum(z2, 0.0) - jnp.log2(1.0 + jnp.exp2(-jnp.abs(z2)))
    return ls, ls - z2


def _split_bf16(x):
    hi = x.astype(BF16)
    lo = (x - hi.astype(F32)).astype(BF16)
    return hi, lo


def _inproj_kernel(x_ref, gmix_ref, w_ref, gq_ref, gk_ref, lng_ref, lnb_ref, ws_ref, bs_ref, gcm_ref,
                   q_ref, kf_ref, kb_ref, vf_ref, vb_ref, ocm_ref, *rest, n_heads, fresh_chunk_rows):
    cs_ref = rest[0] if fresh_chunk_rows else None
    xn_ref, u_ref = rest[-2:]
    j = pl.program_id(1)

    @pl.when(j == 0)
    def _():
        xn_ref[...] = (_rms_rows(x_ref[...]) * gmix_ref[...]).astype(BF16)

    z = jnp.dot(xn_ref[...], w_ref[...], preferred_element_type=F32)
    tm = z.shape[0]

    def head(a, h):
        return a[:, h * HEAD_DIM:(h + 1) * HEAD_DIM]

    @pl.when(j == 0)
    def _():
        for h in range(n_heads):
            q = _rms_rows(head(z, h)) * gq_ref[...]
            q_ref[:, h * HEAD_DIM:(h + 1) * HEAD_DIM] = (q * Q_SCALE).astype(q_ref.dtype)

    @pl.when(j == 1)
    def _():
        for h in range(n_heads):
            k = _rms_rows(head(z, h)) * gk_ref[...]
            kf_ref[:, h * HEAD_DIM:(h + 1) * HEAD_DIM] = k
            kb_ref[:, h * HEAD_DIM:(h + 1) * HEAD_DIM] = k.astype(BF16)

    @pl.when(j == 2)
    def _():
        vf_ref[...] = z
        vb_ref[...] = z.astype(BF16)

    @pl.when(j == 3)
    def _():
        u_ref[...] = _gelu(z)

    @pl.when(j == 4)
    def _():
        g = _gelu(z)
        if not fresh_chunk_rows:
            row = lax.broadcasted_iota(jnp.int32, (CHUNK, CHUNK), 0)
            col = lax.broadcasted_iota(jnp.int32, (CHUNK, CHUNK), 1)
            tril = col <= row
        for h in range(n_heads):
            gh = head(g, h)
            mu = jnp.mean(gh, axis=-1, keepdims=True)
            xc = gh - mu
            vc = xc * lax.rsqrt(jnp.mean(xc * xc, axis=-1, keepdims=True) + EPS)
            vc = vc * lng_ref[:, h * HEAD_DIM:(h + 1) * HEAD_DIM] + lnb_ref[:, h * HEAD_DIM:(h + 1) * HEAD_DIM]
            if fresh_chunk_rows:
                cs_ref[:, h * HEAD_DIM:(h + 1) * HEAD_DIM] = vc
                mixed = ws_ref[h, 0:1, 0:1] * vc + bs_ref[0:1, h:h + 1]
                u_ref[:, h * HEAD_DIM:(h + 1) * HEAD_DIM] = u_ref[:, h * HEAD_DIM:(h + 1) * HEAD_DIM] * mixed
            else:
                wm = jnp.where(tril, ws_ref[h], 0.0).astype(BF16)
                for c in range(tm // CHUNK):
                    rows = slice(c * CHUNK, (c + 1) * CHUNK)
                    mixed = jnp.dot(wm, vc[rows].astype(BF16), preferred_element_type=F32)
                    mixed = mixed + bs_ref[:, h:h + 1]
                    u_ref[rows, h * HEAD_DIM:(h + 1) * HEAD_DIM] = (
                        u_ref[rows, h * HEAD_DIM:(h + 1) * HEAD_DIM] * mixed)
        ocm_ref[...] = (_rms_rows(u_ref[...]) * gcm_ref[...]).astype(BF16)


def _in_proj(x, g_mix, w_in_b, g_q, g_k, ln_g, ln_b, w_s, b_s_t, g_out_cm, *, tm, fresh_chunk_rows):
    T, d_model = x.shape
    n_heads = w_s.shape[0]
    gw = n_heads * HEAD_DIM
    assert w_in_b.shape == (d_model, 5 * gw) and T % tm == 0
    assert fresh_chunk_rows or tm % CHUNK == 0
    row_blk = lambda i, j: (i, 0)
    const2 = lambda i, j: (0, 0)
    out_f32 = jax.ShapeDtypeStruct((T, gw), F32)
    out_b16 = jax.ShapeDtypeStruct((T, gw), BF16)
    rows = lambda: pl.BlockSpec((tm, gw), row_blk)
    out_shape = [out_f32 if fresh_chunk_rows else out_b16, out_f32, out_b16, out_f32, out_b16, out_b16]
    if fresh_chunk_rows:
        out_shape.append(out_f32)
    return pl.pallas_call(
        functools.partial(_inproj_kernel, n_heads=n_heads, fresh_chunk_rows=fresh_chunk_rows),
        grid=(T // tm, 5),
        in_specs=[
            pl.BlockSpec((tm, d_model), row_blk),
            pl.BlockSpec((1, d_model), const2),
            pl.BlockSpec((d_model, gw), lambda i, j: (0, j)),
            pl.BlockSpec((1, HEAD_DIM), const2),
            pl.BlockSpec((1, HEAD_DIM), const2),
            pl.BlockSpec((1, gw), const2),
            pl.BlockSpec((1, gw), const2),
            pl.BlockSpec((n_heads, CHUNK, CHUNK), lambda i, j: (0, 0, 0)),
            pl.BlockSpec((CHUNK, n_heads), const2),
            pl.BlockSpec((1, gw), const2),
        ],
        out_specs=[rows() for _ in out_shape],
        out_shape=out_shape,
        scratch_shapes=[pltpu.VMEM((tm, d_model), BF16), pltpu.VMEM((tm, gw), F32)],
        compiler_params=_cparams("arbitrary", "arbitrary"),
        name="in_proj",
    )(x, g_mix, w_in_b, g_q, g_k, ln_g, ln_b, w_s, b_s_t, g_out_cm)


def _sb_prompt_kernel(bias_ref, q_ref, k_ref, v_ref, o_ref, *, blk, heads_per_step):
    hg = pl.program_id(1)
    i = pl.program_id(2)
    row = lax.broadcasted_iota(jnp.int32, (blk, blk), 0)
    col = lax.broadcasted_iota(jnp.int32, (blk, blk), 1)
    later = jnp.where(row > col, 1.0, 0.0).astype(BF16)
    causal = col < row

    def block(j, runs, on_diagonal):
        start = pl.multiple_of(j * blk, blk)
        heads = range(heads_per_step)
        cols = [slice(g * HEAD_DIM, (g + 1) * HEAD_DIM) for g in heads]
        zs = [lax.dot_general(q_ref[0, :, cols[g]], k_ref[0, pl.ds(start, blk), cols[g]],
                              (((1,), (1,)), ((), ())), preferred_element_type=F32)
              + bias_ref[hg * heads_per_step + g] * LOG2E for g in heads]
        lss, lks = [], []
        for g in heads:
            ls, lk = _log2_sigmoid_pair(zs[g])
            lss.append(ls)
            lks.append(jnp.where(causal, lk, 0.0) if on_diagonal else lk)
        afters = [jnp.dot(lk.astype(BF16), later, preferred_element_type=F32) for lk in lks]
        weights = []
        for g in heads:
            a = jnp.exp2(lss[g] + afters[g] + runs[g])
            weights.append((jnp.where(causal, a, 0.0) if on_diagonal else a).astype(BF16))
        for g in heads:
            pv = jnp.dot(weights[g], v_ref[0, pl.ds(start, blk), cols[g]], preferred_element_type=F32)
            if on_diagonal:
                o_ref[0, :, cols[g]] = pv
            else:
                o_ref[0, :, cols[g]] += pv
        return tuple(runs[g] + jnp.sum(lks[g], axis=-1, keepdims=True) for g in heads)

    runs = block(i, (jnp.zeros((blk, 1), F32),) * heads_per_step, True)
    lax.fori_loop(0, i, lambda t, runs: block(i - 1 - t, runs, False), runs)


def _sb_prompt(q_b, k_b, v_b, sb_bias, *, blk, heads_per_step):
    B, L, sbw = q_b.shape
    n_heads = sbw // HEAD_DIM
    assert L % blk == 0 and n_heads % heads_per_step == 0
    gw = heads_per_step * HEAD_DIM
    qo_spec = lambda: pl.BlockSpec((1, blk, gw), lambda b, h, i, bias: (b, i, h))
    kv_spec = lambda: pl.BlockSpec((1, L, gw), lambda b, h, i, bias: (b, 0, h))
    return pl.pallas_call(
        functools.partial(_sb_prompt_kernel, blk=blk, heads_per_step=heads_per_step),
        grid_spec=pltpu.PrefetchScalarGridSpec(
            num_scalar_prefetch=1,
            grid=(B, n_heads // heads_per_step, L // blk),
            in_specs=[qo_spec(), kv_spec(), kv_spec()],
            out_specs=qo_spec(),
        ),
        out_shape=jax.ShapeDtypeStruct((B, L, sbw), F32),
        compiler_params=_cparams("arbitrary", "arbitrary", "arbitrary"),
        name="sb_prompt",
    )(sb_bias, q_b, k_b, v_b)


PAGES_PER_STEP = 8
SUBLANES = 8


def _sb_decode_kernel(pt_ref, q_ref, bias_ref, *refs):
    k_refs = refs[:PAGES_PER_STEP]
    v_refs = refs[PAGES_PER_STEP:2 * PAGES_PER_STEP]
    o_ref, run_ref, acc_ref = refs[2 * PAGES_PER_STEP:]
    c = pl.program_id(1)
    page, n_heads, hd = k_refs[0].shape
    n = page * n_heads

    @pl.when(c == 0)
    def _():
        run_ref[...] = jnp.zeros_like(run_ref)
        acc_ref[...] = jnp.zeros_like(acc_ref)

    qm = q_ref[0].astype(BF16)
    bias2 = bias_ref[...] * LOG2E
    lane = lax.broadcasted_iota(jnp.int32, (n_heads, n), 1)
    own = lane % n_heads == lax.broadcasted_iota(jnp.int32, (n_heads, n), 0)
    def page_terms(r):
        kp = k_refs[r][...].reshape(n, hd).astype(BF16)
        z = lax.dot_general(qm, kp, (((1,), (1,)), ((), ())), preferred_element_type=F32) + bias2
        ls, lk = _log2_sigmoid_pair(z)
        suffix, total = lk, lk
        shift = n_heads
        while shift < n:
            ahead = pltpu.roll(suffix, n - shift, axis=1)
            suffix = suffix + jnp.where(lane < n - shift, ahead, 0.0)
            total = total + pltpu.roll(total, shift, axis=1)
            shift *= 2
        return ls + (suffix - lk), total

    terms = [page_terms(r) for r in reversed(range(PAGES_PER_STEP))]
    run = run_ref[...]
    acc = acc_ref[...]
    for r, (log_w, total) in zip(reversed(range(PAGES_PER_STEP)), terms):
        a = jnp.where(own, jnp.exp2(log_w + run), 0.0)
        run = run + total
        hi, lo = _split_bf16(a)
        vp = v_refs[r][...].reshape(n, hd).astype(BF16)
        o2 = jnp.dot(jnp.concatenate([hi, lo], axis=0), vp, preferred_element_type=F32)
        acc = acc + (o2[:n_heads] + o2[n_heads:])
    run_ref[...] = run
    acc_ref[...] = acc

    @pl.when(c == pl.num_programs(1) - 1)
    def _():
        o_ref[0] = acc


def _sb_decode(q, cache_k, cache_v, layer, page_table, sb_bias):
    B, sbw = q.shape
    _, _, page, n_heads, hd = cache_k.shape
    n_pages = page_table.shape[1]
    assert n_pages % PAGES_PER_STEP == 0 and n_heads == SUBLANES and hd == HEAD_DIM
    n_steps = n_pages // PAGES_PER_STEP
    n = page * n_heads
    assert n & (n - 1) == 0
    bias = jnp.broadcast_to(sb_bias[:, None], (n_heads, n))

    def page_spec(r):
        return pl.BlockSpec(
            (None, None, page, n_heads, hd),
            lambda b, c, pt: (layer, pt[b, (n_steps - 1 - c) * PAGES_PER_STEP + r], 0, 0, 0))

    head_rows = lambda: pl.BlockSpec((1, n_heads, hd), lambda b, c, pt: (b, 0, 0))
    out = pl.pallas_call(
        _sb_decode_kernel,
        grid_spec=pltpu.PrefetchScalarGridSpec(
            num_scalar_prefetch=1,
            grid=(B, n_steps),
            in_specs=[head_rows(), pl.BlockSpec((n_heads, n), lambda b, c, pt: (0, 0))]
                     + [page_spec(r) for r in range(PAGES_PER_STEP)] * 2,
            out_specs=head_rows(),
            scratch_shapes=[pltpu.VMEM((n_heads, n), F32), pltpu.VMEM((n_heads, hd), F32)],
        ),
        out_shape=jax.ShapeDtypeStruct((B, n_heads, hd), F32),
        compiler_params=_cparams("arbitrary", "arbitrary"),
        name="sb_decode",
    )(page_table, q.reshape(B, n_heads, hd), bias,
      *([cache_k] * PAGES_PER_STEP), *([cache_v] * PAGES_PER_STEP))
    return out.reshape(B, sbw)


def _out_proj_kernel(osb_ref, ocm_ref, h_ref, gsb_ref, w_ref, gffn_ref, h1_ref, f_ref, o_scr):
    sbw = osb_ref.shape[1]
    o_scr[:, :sbw] = (_rms_rows(osb_ref[...]) * gsb_ref[...]).astype(BF16)
    o_scr[:, sbw:] = ocm_ref[...]
    h1 = h_ref[...] + jnp.dot(o_scr[...], w_ref[...], preferred_element_type=F32)
    h1_ref[...] = h1
    f_ref[...] = (_rms_rows(h1) * gffn_ref[...]).astype(BF16)


def _out_proj(o_sb, o_cm_n, h, g_out_sb, w_out_b, g_ffn, *, tm):
    T, d_model = h.shape
    sbw, cmw = o_sb.shape[1], o_cm_n.shape[1]
    assert T % tm == 0 and w_out_b.shape == (sbw + cmw, d_model)
    row_blk = lambda i: (i, 0)
    const2 = lambda i: (0, 0)
    return pl.pallas_call(
        _out_proj_kernel,
        grid=(T // tm,),
        in_specs=[pl.BlockSpec((tm, sbw), row_blk),
                  pl.BlockSpec((tm, cmw), row_blk),
                  pl.BlockSpec((tm, d_model), row_blk),
                  pl.BlockSpec((1, sbw), const2),
                  pl.BlockSpec((sbw + cmw, d_model), const2),
                  pl.BlockSpec((1, d_model), const2)],
        out_specs=[pl.BlockSpec((tm, d_model), row_blk), pl.BlockSpec((tm, d_model), row_blk)],
        out_shape=[jax.ShapeDtypeStruct((T, d_model), F32), jax.ShapeDtypeStruct((T, d_model), BF16)],
        scratch_shapes=[pltpu.VMEM((tm, sbw + cmw), BF16)],
        compiler_params=_cparams("arbitrary"),
        name="out_proj",
    )(o_sb, o_cm_n, h, g_out_sb, w_out_b, g_ffn)


def _ffn_kernel(f_ref, h1_ref, wg_ref, wu_ref, wd_ref, gple_ref, h2_ref, hn_ref, acc_ref):
    j = pl.program_id(1)

    @pl.when(j == 0)
    def _():
        acc_ref[...] = h1_ref[...]

    f = f_ref[...]
    gate = jnp.dot(f, wg_ref[...], preferred_element_type=F32)
    up = jnp.dot(f, wu_ref[...], preferred_element_type=F32)
    act = (gate * jax.nn.sigmoid(gate) * up).astype(BF16)
    acc_ref[...] += jnp.dot(act, wd_ref[...], preferred_element_type=F32)

    @pl.when(j == pl.num_programs(1) - 1)
    def _():
        h2 = acc_ref[...]
        h2_ref[...] = h2
        hn_ref[...] = (_rms_rows(h2) * gple_ref[...]).astype(BF16)


def _ffn(f, h1, wg_b, wu_b, wd_b, g_ple, *, tm, tf):
    T, d_model = h1.shape
    d_ff = wg_b.shape[1]
    assert T % tm == 0 and d_ff % tf == 0
    row_blk = lambda i, j: (i, 0)
    return pl.pallas_call(
        _ffn_kernel,
        grid=(T // tm, d_ff // tf),
        in_specs=[pl.BlockSpec((tm, d_model), row_blk),
                  pl.BlockSpec((tm, d_model), row_blk),
                  pl.BlockSpec((d_model, tf), lambda i, j: (0, j)),
                  pl.BlockSpec((d_model, tf), lambda i, j: (0, j)),
                  pl.BlockSpec((tf, d_model), lambda i, j: (j, 0)),
                  pl.BlockSpec((1, d_model), lambda i, j: (0, 0))],
        out_specs=[pl.BlockSpec((tm, d_model), row_blk), pl.BlockSpec((tm, d_model), row_blk)],
        out_shape=[jax.ShapeDtypeStruct((T, d_model), F32), jax.ShapeDtypeStruct((T, d_model), BF16)],
        scratch_shapes=[pltpu.VMEM((tm, d_model), F32)],
        compiler_params=_cparams("arbitrary", "arbitrary"),
        name="ffn",
    )(f, h1, wg_b, wu_b, wd_b, g_ple)


def _ple_kernel(hn_ref, h2_ref, p_ref, wg_ref, wp_ref, o_ref):
    gate = jax.nn.sigmoid(jnp.dot(hn_ref[...], wg_ref[...], preferred_element_type=F32))
    proj = jnp.dot(p_ref[...].astype(BF16), wp_ref[...], preferred_element_type=F32)
    o_ref[...] = h2_ref[...] + gate * proj


def _ple(hn, h2, p, wpg_b, wpp_b, *, tm):
    T, d_model = h2.shape
    ple_dim = p.shape[1]
    assert T % tm == 0
    row_blk = lambda i: (i, 0)
    const2 = lambda i: (0, 0)
    return pl.pallas_call(
        _ple_kernel,
        grid=(T // tm,),
        in_specs=[pl.BlockSpec((tm, d_model), row_blk),
                  pl.BlockSpec((tm, d_model), row_blk),
                  pl.BlockSpec((tm, ple_dim), row_blk),
                  pl.BlockSpec((d_model, d_model), const2),
                  pl.BlockSpec((ple_dim, d_model), const2)],
        out_specs=pl.BlockSpec((tm, d_model), row_blk),
        out_shape=jax.ShapeDtypeStruct((T, d_model), F32),
        compiler_params=_cparams("arbitrary"),
        name="ple",
    )(hn, h2, p, wpg_b, wpp_b)


def _layer_tail(h, o_sb, o_cm_n, p, g_out_sb, w_out_b, g_ffn, wg_b, wu_b, wd_b, g_ple, wpg_b, wpp_b,
                *, tm_proj, tm_ffn, tf):
    h1, f = _out_proj(o_sb, o_cm_n, h, g_out_sb, w_out_b, g_ffn, tm=tm_proj)
    h2, hn = _ffn(f, h1, wg_b, wu_b, wd_b, g_ple, tm=tm_ffn, tf=tf)
    return _ple(hn, h2, p, wpg_b, wpp_b, tm=tm_proj)


def kernel(x_prompt, x_sample, cache_k, cache_v, page_table, p_prompt, p_sample, g_mix, w_in, g_q, g_k, sb_bias, sgu_ln_g, sgu_ln_b, w_s, b_s, g_out_sb, g_out_cm, w_out, g_ffn, w_gate, w_up, w_down, g_ple, w_ple_gate, w_ple_proj):
    depth = w_in.shape[0]
    B, L, d_model = x_prompt.shape
    Bs, Ls, _ = x_sample.shape
    n_sb = cache_k.shape[3]
    n_cm = w_s.shape[1]
    sbw, cmw = n_sb * HEAD_DIM, n_cm * HEAD_DIM
    assert Ls == 1 and n_sb == n_cm and cache_k.shape[4] == HEAD_DIM
    assert (page_table.shape[1] * cache_k.shape[2]) % CHUNK == 0 and L % CHUNK == 0

    hp = x_prompt.reshape(B * L, d_model)
    hs = x_sample.reshape(Bs * Ls, d_model)
    row = lambda a: a.reshape(1, -1)
    kp_l, vp_l, ks_l, vs_l, cs_l = [], [], [], [], []
    for i in range(depth):
        w_in_b = w_in[i].astype(BF16)
        w_out_b = w_out[i].astype(BF16)
        wg_b, wu_b, wd_b = w_gate[i].astype(BF16), w_up[i].astype(BF16), w_down[i].astype(BF16)
        wpg_b, wpp_b = w_ple_gate[i].astype(BF16), w_ple_proj[i].astype(BF16)
        proj_args = (row(g_mix[i]), w_in_b, row(g_q[i]), row(g_k[i]), row(sgu_ln_g[i]), row(sgu_ln_b[i]),
                     w_s[i], b_s[i].T, row(g_out_cm[i]))
        tail_args = (row(g_out_sb[i]), w_out_b, row(g_ffn[i]), wg_b, wu_b, wd_b, row(g_ple[i]), wpg_b, wpp_b)

        q_b, k_f, k_b, v_f, v_b, ocm_n = _in_proj(hp, *proj_args, tm=512, fresh_chunk_rows=False)
        o_sb = _sb_prompt(q_b.reshape(B, L, sbw), k_b.reshape(B, L, sbw), v_b.reshape(B, L, sbw),
                          sb_bias[i], blk=256, heads_per_step=4)
        hp = _layer_tail(hp, o_sb.reshape(B * L, sbw), ocm_n, p_prompt[i].reshape(B * L, -1), *tail_args,
                         tm_proj=256, tm_ffn=512, tf=512)
        kp_l.append(k_f.reshape(B, L, n_sb, HEAD_DIM))
        vp_l.append(v_f.reshape(B, L, n_sb, HEAD_DIM))

        q_s, ks_f, _, vs_f, _, ocm_s, cs = _in_proj(hs, *proj_args, tm=Bs, fresh_chunk_rows=True)
        o_sb_s = _sb_decode(q_s, cache_k, cache_v, i, page_table, sb_bias[i])
        hs = _layer_tail(hs, o_sb_s, ocm_s, p_sample[i].reshape(Bs, -1), *tail_args,
                         tm_proj=Bs, tm_ffn=Bs, tf=512)
        ks_l.append(ks_f.reshape(Bs, Ls, n_sb, HEAD_DIM))
        vs_l.append(vs_f.reshape(Bs, Ls, n_sb, HEAD_DIM))
        cs_l.append(cs.reshape(Bs, Ls, n_cm, HEAD_DIM))

    return (hp.reshape(B, L, d_model), hs.reshape(Bs, Ls, d_model),
            jnp.stack(kp_l), jnp.stack(vp_l), jnp.stack(ks_l), jnp.stack(vs_l), jnp.stack(cs_l))
```

```python
import functools
import math

import jax
import jax.numpy as jnp
from jax import lax
from jax.experimental import pallas as pl
from jax.experimental.pallas import tpu as pltpu

HEAD_DIM = 128
CHUNK = 128
EPS = 1e-6
SB_SCALE = 1.0 / math.sqrt(HEAD_DIM)
INV_SQRT2 = 0.7071067811865476
LOG2E = 1.4426950408889634
Q_SCALE = SB_SCALE * LOG2E

LANES = 128
VMEM_LIMIT = 56 * 1024 * 1024

F32 = jnp.float32
BF16 = jnp.bfloat16


def _cparams(*sem):
    return pltpu.CompilerParams(dimension_semantics=sem, vmem_limit_bytes=VMEM_LIMIT)


def _rms_rows(x):
    return x * lax.rsqrt(jnp.mean(x * x, axis=-1, keepdims=True) + EPS)


def _gelu(x):
    return 0.5 * x * (1.0 + lax.erf(x * INV_SQRT2))


def _log2_sigmoid_pair(z2):
    ls = jnp.minimum(z2, 0.0) - jnp.log2(1.0 + jnp.exp2(-jnp.abs(z2)))
    return ls, ls - z2


def _split_bf16(x):
    hi = x.astype(BF16)
    lo = (x - hi.astype(F32)).astype(BF16)
    return hi, lo


def _inproj_kernel(x_ref, gmix_ref, w_ref, gq_ref, gk_ref, lng_ref, lnb_ref, ws_ref, bs_ref, gcm_ref,
                   q_ref, kf_ref, kb_ref, vf_ref, vb_ref, ocm_ref, *rest, n_heads, fresh_chunk_rows):
    cs_ref = rest[0] if fresh_chunk_rows else None
    xn_ref, u_ref = rest[-2:]
    j = pl.program_id(1)

    @pl.when(j == 0)
    def _():
        xn_ref[...] = (_rms_rows(x_ref[...]) * gmix_ref[...]).astype(BF16)

    tm = xn_ref.shape[0]
    heads_per_dot = 4

    def head_pairs():
        slab = heads_per_dot * HEAD_DIM
        for p in range(n_heads // heads_per_dot):
            zp = jnp.dot(xn_ref[...], w_ref[:, p * slab:(p + 1) * slab], preferred_element_type=F32)
            for hh in range(heads_per_dot):
                yield heads_per_dot * p + hh, zp[:, hh * HEAD_DIM:(hh + 1) * HEAD_DIM]

    def cols(h):
        return slice(h * HEAD_DIM, (h + 1) * HEAD_DIM)

    @pl.when(j == 0)
    def _():
        for h, z in head_pairs():
            q = _rms_rows(z) * gq_ref[...]
            q_ref[:, cols(h)] = (q * Q_SCALE).astype(q_ref.dtype)

    @pl.when(j == 1)
    def _():
        for h, z in head_pairs():
            k = _rms_rows(z) * gk_ref[...]
            kf_ref[:, cols(h)] = k
            kb_ref[:, cols(h)] = k.astype(BF16)

    @pl.when(j == 2)
    def _():
        for h, z in head_pairs():
            vf_ref[:, cols(h)] = z
            vb_ref[:, cols(h)] = z.astype(BF16)

    @pl.when(j == 3)
    def _():
        for h, z in head_pairs():
            u_ref[:, cols(h)] = _gelu(z)

    @pl.when(j == 4)
    def _():
        if not fresh_chunk_rows:
            row = lax.broadcasted_iota(jnp.int32, (CHUNK, CHUNK), 0)
            col = lax.broadcasted_iota(jnp.int32, (CHUNK, CHUNK), 1)
            tril = col <= row
        for h, z in head_pairs():
            gh = _gelu(z)
            mu = jnp.mean(gh, axis=-1, keepdims=True)
            xc = gh - mu
            vc = xc * lax.rsqrt(jnp.mean(xc * xc, axis=-1, keepdims=True) + EPS)
            vc = vc * lng_ref[:, cols(h)] + lnb_ref[:, cols(h)]
            if fresh_chunk_rows:
                cs_ref[:, cols(h)] = vc
                mixed = ws_ref[h, 0:1, 0:1] * vc + bs_ref[0:1, h:h + 1]
                u_ref[:, cols(h)] = u_ref[:, cols(h)] * mixed
            else:
                wm = jnp.where(tril, ws_ref[h], 0.0).astype(BF16)
                for c in range(tm // CHUNK):
                    rows = slice(c * CHUNK, (c + 1) * CHUNK)
                    mixed = jnp.dot(wm, vc[rows].astype(BF16), preferred_element_type=F32)
                    mixed = mixed + bs_ref[:, h:h + 1]
                    u_ref[rows, cols(h)] = u_ref[rows, cols(h)] * mixed
        ocm_ref[...] = (_rms_rows(u_ref[...]) * gcm_ref[...]).astype(BF16)


def _in_proj(x, g_mix, w_in_b, g_q, g_k, ln_g, ln_b, w_s, b_s_t, g_out_cm, *, tm, fresh_chunk_rows):
    T, d_model = x.shape
    n_heads = w_s.shape[0]
    gw = n_heads * HEAD_DIM
    assert w_in_b.shape == (d_model, 5 * gw) and T % tm == 0
    assert fresh_chunk_rows or tm % CHUNK == 0
    row_blk = lambda i, j: (i, 0)
    const2 = lambda i, j: (0, 0)
    out_f32 = jax.ShapeDtypeStruct((T, gw), F32)
    out_b16 = jax.ShapeDtypeStruct((T, gw), BF16)
    rows = lambda: pl.BlockSpec((tm, gw), row_blk)
    out_shape = [out_f32 if fresh_chunk_rows else out_b16, out_f32, out_b16, out_f32, out_b16, out_b16]
    if fresh_chunk_rows:
        out_shape.append(out_f32)
    return pl.pallas_call(
        functools.partial(_inproj_kernel, n_heads=n_heads, fresh_chunk_rows=fresh_chunk_rows),
        grid=(T // tm, 5),
        in_specs=[
            pl.BlockSpec((tm, d_model), row_blk),
            pl.BlockSpec((1, d_model), const2),
            pl.BlockSpec((d_model, gw), lambda i, j: (0, j)),
            pl.BlockSpec((1, HEAD_DIM), const2),
            pl.BlockSpec((1, HEAD_DIM), const2),
            pl.BlockSpec((1, gw), const2),
            pl.BlockSpec((1, gw), const2),
            pl.BlockSpec((n_heads, CHUNK, CHUNK), lambda i, j: (0, 0, 0)),
            pl.BlockSpec((CHUNK, n_heads), const2),
            pl.BlockSpec((1, gw), const2),
        ],
        out_specs=[rows() for _ in out_shape],
        out_shape=out_shape,
        scratch_shapes=[pltpu.VMEM((tm, d_model), BF16), pltpu.VMEM((tm, gw), F32)],
        compiler_params=_cparams("arbitrary", "arbitrary"),
        name="in_proj",
    )(x, g_mix, w_in_b, g_q, g_k, ln_g, ln_b, w_s, b_s_t, g_out_cm)


def _sb_prompt_kernel(bias_ref, q_ref, k_ref, v_ref, o_ref, *, blk, heads_per_step):
    hg = pl.program_id(1)
    i = pl.program_id(2)
    row = lax.broadcasted_iota(jnp.int32, (blk, blk), 0)
    col = lax.broadcasted_iota(jnp.int32, (blk, blk), 1)
    later = jnp.where(row > col, 1.0, 0.0).astype(BF16)
    causal = col < row

    def block(j, runs, on_diagonal):
        start = pl.multiple_of(j * blk, blk)
        heads = range(heads_per_step)
        cols = [slice(g * HEAD_DIM, (g + 1) * HEAD_DIM) for g in heads]
        zs = [lax.dot_general(q_ref[0, :, cols[g]], k_ref[0, pl.ds(start, blk), cols[g]],
                              (((1,), (1,)), ((), ())), preferred_element_type=F32)
              + bias_ref[hg * heads_per_step + g] * LOG2E for g in heads]
        lss, lks = [], []
        for g in heads:
            ls, lk = _log2_sigmoid_pair(zs[g])
            lss.append(ls)
            lks.append(jnp.where(causal, lk, 0.0) if on_diagonal else lk)
        afters = [jnp.dot(lk.astype(BF16), later, preferred_element_type=F32) for lk in lks]
        weights = []
        for g in heads:
            a = jnp.exp2(lss[g] + afters[g] + runs[g])
            weights.append((jnp.where(causal, a, 0.0) if on_diagonal else a).astype(BF16))
        for g in heads:
            pv = jnp.dot(weights[g], v_ref[0, pl.ds(start, blk), cols[g]], preferred_element_type=F32)
            if on_diagonal:
                o_ref[0, :, cols[g]] = pv
            else:
                o_ref[0, :, cols[g]] += pv
        return tuple(runs[g] + jnp.sum(lks[g], axis=-1, keepdims=True) for g in heads)

    runs = block(i, (jnp.zeros((blk, 1), F32),) * heads_per_step, True)
    lax.fori_loop(0, i, lambda t, runs: block(i - 1 - t, runs, False), runs)


def _sb_prompt(q_b, k_b, v_b, sb_bias, *, blk, heads_per_step):
    B, L, sbw = q_b.shape
    n_heads = sbw // HEAD_DIM
    assert L % blk == 0 and n_heads % heads_per_step == 0
    gw = heads_per_step * HEAD_DIM
    qo_spec = lambda: pl.BlockSpec((1, blk, gw), lambda b, h, i, bias: (b, i, h))
    kv_spec = lambda: pl.BlockSpec((1, L, gw), lambda b, h, i, bias: (b, 0, h))
    return pl.pallas_call(
        functools.partial(_sb_prompt_kernel, blk=blk, heads_per_step=heads_per_step),
        grid_spec=pltpu.PrefetchScalarGridSpec(
            num_scalar_prefetch=1,
            grid=(B, n_heads // heads_per_step, L // blk),
            in_specs=[qo_spec(), kv_spec(), kv_spec()],
            out_specs=qo_spec(),
        ),
        out_shape=jax.ShapeDtypeStruct((B, L, sbw), F32),
        compiler_params=_cparams("arbitrary", "arbitrary", "arbitrary"),
        name="sb_prompt",
    )(sb_bias, q_b, k_b, v_b)


SUBLANES = 8


def _page_weights(qm, bias2, k_ref):
    page, n_heads, hd = k_ref.shape
    n = page * n_heads
    lane = lax.broadcasted_iota(jnp.int32, (n_heads, n), 1)
    sub = lax.broadcasted_iota(jnp.int32, (n_heads, n), 0)
    kp = k_ref[...].reshape(n, hd).astype(BF16)
    z = lax.dot_general(qm, kp, (((1,), (1,)), ((), ())), preferred_element_type=F32) + bias2
    ls, lk = _log2_sigmoid_pair(z)
    suffix = lk
    shift = n_heads
    while shift < n:
        ahead = pltpu.roll(suffix, n - shift, axis=1)
        suffix = suffix + jnp.where(lane < n - shift, ahead, 0.0)
        shift *= 2
    total = jnp.sum(jnp.where(lane == sub, suffix, 0.0), axis=1, keepdims=True)
    a = jnp.where(lane % n_heads == sub, jnp.exp2(ls + (suffix - lk)), 0.0)
    hi, lo = _split_bf16(a)
    return jnp.concatenate([hi, lo], axis=0), total


def _page_output(weights, v_ref):
    page, n_heads, hd = v_ref.shape
    o2 = jnp.dot(weights, v_ref[...].reshape(page * n_heads, hd).astype(BF16), preferred_element_type=F32)
    return o2[:n_heads] + o2[n_heads:]


def _decode_combine_kernel(pacc_ref, ptot_ref, o_ref):
    n_pages = pacc_ref.shape[0]

    def body(t, carry):
        run, acc = carry
        p = n_pages - 1 - t
        return run + ptot_ref[p], acc + jnp.exp2(run) * pacc_ref[p]

    zero = jnp.zeros(o_ref.shape[1:], F32)
    _, acc = lax.fori_loop(0, n_pages, body, (zero, zero))
    o_ref[0] = acc


def _decode_combine(pacc, ptot, n_seq, pages_per_seq):
    _, n_heads, hd = pacc.shape
    seq_pages = lambda: pl.BlockSpec((pages_per_seq, n_heads, hd), lambda b: (b, 0, 0))
    out = pl.pallas_call(
        _decode_combine_kernel,
        grid=(n_seq,),
        in_specs=[seq_pages(), seq_pages()],
        out_specs=pl.BlockSpec((1, n_heads, hd), lambda b: (b, 0, 0)),
        out_shape=jax.ShapeDtypeStruct((n_seq, n_heads, hd), F32),
        compiler_params=_cparams("arbitrary"),
        name="sb_decode_combine",
    )(pacc, ptot)
    return out.reshape(n_seq, n_heads * hd)


def _out_proj_kernel(osb_ref, ocm_ref, h_ref, gsb_ref, w_ref, gffn_ref, h1_ref, f_ref, o_scr):
    sbw = osb_ref.shape[1]
    o_scr[:, :sbw] = (_rms_rows(osb_ref[...]) * gsb_ref[...]).astype(BF16)
    o_scr[:, sbw:] = ocm_ref[...]
    h1 = h_ref[...] + jnp.dot(o_scr[...], w_ref[...], preferred_element_type=F32)
    h1_ref[...] = h1
    f_ref[...] = (_rms_rows(h1) * gffn_ref[...]).astype(BF16)


def _out_proj(o_sb, o_cm_n, h, g_out_sb, w_out_b, g_ffn, *, tm):
    T, d_model = h.shape
    sbw, cmw = o_sb.shape[1], o_cm_n.shape[1]
    assert T % tm == 0 and w_out_b.shape == (sbw + cmw, d_model)
    row_blk = lambda i: (i, 0)
    const2 = lambda i: (0, 0)
    return pl.pallas_call(
        _out_proj_kernel,
        grid=(T // tm,),
        in_specs=[pl.BlockSpec((tm, sbw), row_blk),
                  pl.BlockSpec((tm, cmw), row_blk),
                  pl.BlockSpec((tm, d_model), row_blk),
                  pl.BlockSpec((1, sbw), const2),
                  pl.BlockSpec((sbw + cmw, d_model), const2),
                  pl.BlockSpec((1, d_model), const2)],
        out_specs=[pl.BlockSpec((tm, d_model), row_blk), pl.BlockSpec((tm, d_model), row_blk)],
        out_shape=[jax.ShapeDtypeStruct((T, d_model), F32), jax.ShapeDtypeStruct((T, d_model), BF16)],
        scratch_shapes=[pltpu.VMEM((tm, sbw + cmw), BF16)],
        compiler_params=_cparams("arbitrary"),
        name="out_proj",
    )(o_sb, o_cm_n, h, g_out_sb, w_out_b, g_ffn)


def _ffn_kernel(*refs, side_pages, pages_per_seq, total_pages):
    if side_pages:
        pt_ref, f_ref, wgu_ref, wd_ref, q_ref, bias_ref = refs[:6]
        k_refs = refs[6:6 + side_pages]
        v_refs = refs[6 + side_pages:6 + 2 * side_pages]
        delta_ref, pacc_ref, ptot_ref = refs[6 + 2 * side_pages:]
    else:
        f_ref, wgu_ref, wd_ref, delta_ref = refs
    tf = wd_ref.shape[0]
    i, j = pl.program_id(0), pl.program_id(1)

    @pl.when(j == 0)
    def _():
        delta_ref[...] = jnp.zeros_like(delta_ref)

    gate_up = jnp.dot(f_ref[...], wgu_ref[...], preferred_element_type=F32)
    gate, up = gate_up[:, :tf], gate_up[:, tf:]
    if side_pages:
        first = (i * pl.num_programs(1) + j) * side_pages
        bias2 = bias_ref[...] * LOG2E
        weights = []
        for r in range(side_pages):
            seq = jnp.minimum(first + r, total_pages - 1) // pages_per_seq
            w, total = _page_weights(q_ref[seq].astype(BF16), bias2, k_refs[r])
            ptot_ref[r] = jnp.broadcast_to(total, ptot_ref.shape[1:])
            weights.append(w)
    act = (gate * jax.nn.sigmoid(gate) * up).astype(BF16)
    delta_ref[...] += jnp.dot(act, wd_ref[...], preferred_element_type=F32)
    if side_pages:
        for r in range(side_pages):
            pacc_ref[r] = _page_output(weights[r], v_refs[r])


def _interleave_tiles(w_gate, w_up, tf):
    d_model, d_ff = w_gate.shape
    both = jnp.stack([w_gate.reshape(d_model, d_ff // tf, tf), w_up.reshape(d_model, d_ff // tf, tf)], axis=2)
    return both.reshape(d_model, 2 * d_ff)


def _ffn(f, wgu_b, wd_b, *, tm, tf, side=None, side_pages=0):
    T, d_model = f.shape
    d_ff = wd_b.shape[0]
    assert T % tm == 0 and d_ff % tf == 0
    grid = (T // tm, d_ff // tf)
    in_specs = [pl.BlockSpec((tm, d_model), lambda i, j, *_: (i, 0),
                             pipeline_mode=pl.Buffered(1) if side is not None else None),
                pl.BlockSpec((d_model, 2 * tf), lambda i, j, *_: (0, j)),
                pl.BlockSpec((tf, d_model), lambda i, j, *_: (j, 0))]
    out_specs = [pl.BlockSpec((tm, d_model), lambda i, j, *_: (i, 0))]
    out_shape = [jax.ShapeDtypeStruct((T, d_model), F32)]
    args = [f, wgu_b, wd_b]
    if side is None:
        return pl.pallas_call(
            functools.partial(_ffn_kernel, side_pages=0, pages_per_seq=0, total_pages=0),
            grid=grid, in_specs=in_specs, out_specs=out_specs, out_shape=out_shape,
            compiler_params=_cparams("arbitrary", "arbitrary"), name="ffn",
        )(*args)[0]

    q, sb_bias, cache_k, cache_v, layer, page_table = side
    n_seq, pages_per_seq = page_table.shape
    _, _, page, n_heads, hd = cache_k.shape
    n = page * n_heads
    total_pages = n_seq * pages_per_seq
    n_steps = grid[0] * grid[1]
    assert n_heads == SUBLANES and hd == HEAD_DIM and n & (n - 1) == 0
    assert n_steps * side_pages >= total_pages

    def page_spec(r):
        def index(i, j, pt):
            g = jnp.minimum((i * grid[1] + j) * side_pages + r, total_pages - 1)
            return (layer, pt[g // pages_per_seq, g % pages_per_seq], 0, 0, 0)
        return pl.BlockSpec((None, None, page, n_heads, hd), index)

    step_pages = lambda: pl.BlockSpec((side_pages, n_heads, hd), lambda i, j, pt: (i * grid[1] + j, 0, 0))
    part = jax.ShapeDtypeStruct((n_steps * side_pages, n_heads, hd), F32)
    delta, pacc, ptot = pl.pallas_call(
        functools.partial(_ffn_kernel, side_pages=side_pages, pages_per_seq=pages_per_seq,
                          total_pages=total_pages),
        grid_spec=pltpu.PrefetchScalarGridSpec(
            num_scalar_prefetch=1,
            grid=grid,
            in_specs=in_specs
                     + [pl.BlockSpec((n_seq, n_heads, hd), lambda i, j, pt: (0, 0, 0)),
                        pl.BlockSpec((n_heads, n), lambda i, j, pt: (0, 0))]
                     + [page_spec(r) for r in range(side_pages)] * 2,
            out_specs=out_specs + [step_pages(), step_pages()],
        ),
        out_shape=out_shape + [part, part],
        compiler_params=_cparams("arbitrary", "arbitrary"),
        name="ffn_with_decode_pages",
    )(page_table, *args, q, jnp.broadcast_to(sb_bias[:, None], (n_heads, n)),
      *([cache_k] * side_pages), *([cache_v] * side_pages))
    return delta, _decode_combine(pacc, ptot, n_seq, pages_per_seq)


def _ple_kernel(h1_ref, delta_ref, p_ref, gple_ref, wg_ref, wp_ref, o_ref):
    h2 = h1_ref[...] + delta_ref[...]
    hn = (_rms_rows(h2) * gple_ref[...]).astype(BF16)
    gate = jax.nn.sigmoid(jnp.dot(hn, wg_ref[...], preferred_element_type=F32))
    proj = jnp.dot(p_ref[...].astype(BF16), wp_ref[...], preferred_element_type=F32)
    o_ref[...] = h2 + gate * proj


def _ple(h1, delta, p, g_ple, wpg_b, wpp_b, *, tm):
    T, d_model = h1.shape
    ple_dim = p.shape[1]
    assert T % tm == 0
    row_blk = lambda i: (i, 0)
    const2 = lambda i: (0, 0)
    return pl.pallas_call(
        _ple_kernel,
        grid=(T // tm,),
        in_specs=[pl.BlockSpec((tm, d_model), row_blk),
                  pl.BlockSpec((tm, d_model), row_blk),
                  pl.BlockSpec((tm, ple_dim), row_blk),
                  pl.BlockSpec((1, d_model), const2),
                  pl.BlockSpec((d_model, d_model), const2),
                  pl.BlockSpec((ple_dim, d_model), const2)],
        out_specs=pl.BlockSpec((tm, d_model), row_blk),
        out_shape=jax.ShapeDtypeStruct((T, d_model), F32),
        compiler_params=_cparams("arbitrary"),
        name="ple",
    )(h1, delta, p, g_ple, wpg_b, wpp_b)


PROMPT_FFN_ROWS = 1024
FFN_COLS = 256


def kernel(x_prompt, x_sample, cache_k, cache_v, page_table, p_prompt, p_sample, g_mix, w_in, g_q, g_k, sb_bias, sgu_ln_g, sgu_ln_b, w_s, b_s, g_out_sb, g_out_cm, w_out, g_ffn, w_gate, w_up, w_down, g_ple, w_ple_gate, w_ple_proj):
    depth = w_in.shape[0]
    B, L, d_model = x_prompt.shape
    Bs, Ls, _ = x_sample.shape
    n_sb = cache_k.shape[3]
    n_cm = w_s.shape[1]
    d_ff = w_gate.shape[2]
    sbw, cmw = n_sb * HEAD_DIM, n_cm * HEAD_DIM
    assert Ls == 1 and n_sb == n_cm and cache_k.shape[4] == HEAD_DIM
    assert (page_table.shape[1] * cache_k.shape[2]) % CHUNK == 0 and L % CHUNK == 0
    ffn_steps = (B * L // PROMPT_FFN_ROWS) * (d_ff // FFN_COLS)
    side_pages = -(-page_table.size // ffn_steps)

    hp = x_prompt.reshape(B * L, d_model)
    hs = x_sample.reshape(Bs * Ls, d_model)
    row = lambda a: a.reshape(1, -1)
    kp_l, vp_l, ks_l, vs_l, cs_l = [], [], [], [], []
    for i in range(depth):
        w_in_b = w_in[i].astype(BF16)
        w_out_b = w_out[i].astype(BF16)
        wgu_b = _interleave_tiles(w_gate[i].astype(BF16), w_up[i].astype(BF16), FFN_COLS)
        wd_b = w_down[i].astype(BF16)
        wpg_b, wpp_b = w_ple_gate[i].astype(BF16), w_ple_proj[i].astype(BF16)
        proj_args = (row(g_mix[i]), w_in_b, row(g_q[i]), row(g_k[i]), row(sgu_ln_g[i]), row(sgu_ln_b[i]),
                     w_s[i], b_s[i].T, row(g_out_cm[i]))
        out_args = (row(g_out_sb[i]), w_out_b, row(g_ffn[i]))
        ple_args = (row(g_ple[i]), wpg_b, wpp_b)

        q_b, k_f, k_b, v_f, v_b, ocm_n = _in_proj(hp, *proj_args, tm=512, fresh_chunk_rows=False)
        q_s, ks_f, _, vs_f, _, ocm_s, cs = _in_proj(hs, *proj_args, tm=Bs, fresh_chunk_rows=True)

        o_sb = _sb_prompt(q_b.reshape(B, L, sbw), k_b.reshape(B, L, sbw), v_b.reshape(B, L, sbw),
                          sb_bias[i], blk=256, heads_per_step=4)
        h1, f = _out_proj(o_sb.reshape(B * L, sbw), ocm_n, hp, *out_args, tm=256)
        delta, o_sb_s = _ffn(f, wgu_b, wd_b, tm=PROMPT_FFN_ROWS, tf=FFN_COLS, side_pages=side_pages,
                             side=(q_s.reshape(Bs, n_sb, HEAD_DIM), sb_bias[i], cache_k, cache_v, i, page_table))
        hp = _ple(h1, delta, p_prompt[i].reshape(B * L, -1), *ple_args, tm=256)
        kp_l.append(k_f.reshape(B, L, n_sb, HEAD_DIM))
        vp_l.append(v_f.reshape(B, L, n_sb, HEAD_DIM))

        h1_s, f_s = _out_proj(o_sb_s, ocm_s, hs, *out_args, tm=Bs)
        delta_s = _ffn(f_s, wgu_b, wd_b, tm=Bs, tf=FFN_COLS)
        hs = _ple(h1_s, delta_s, p_sample[i].reshape(Bs, -1), *ple_args, tm=Bs)
        ks_l.append(ks_f.reshape(Bs, Ls, n_sb, HEAD_DIM))
        vs_l.append(vs_f.reshape(Bs, Ls, n_sb, HEAD_DIM))
        cs_l.append(cs.reshape(Bs, Ls, n_cm, HEAD_DIM))

    return (hp.reshape(B, L, d_model), hs.reshape(Bs, Ls, d_model),
            jnp.stack(kp_l), jnp.stack(vp_l), jnp.stack(ks_l), jnp.stack(vs_l), jnp.stack(cs_l))
```

```python
import functools
import math

import jax
import jax.numpy as jnp
from jax import lax
from jax.experimental import pallas as pl
from jax.experimental.pallas import tpu as pltpu

HEAD_DIM = 128
CHUNK = 128
EPS = 1e-6
SB_SCALE = 1.0 / math.sqrt(HEAD_DIM)
INV_SQRT2 = 0.7071067811865476
LOG2E = 1.4426950408889634
Q_SCALE = SB_SCALE * LOG2E

LANES = 128
VMEM_LIMIT = 56 * 1024 * 1024

F32 = jnp.float32
BF16 = jnp.bfloat16


def _cparams(*sem):
    return pltpu.CompilerParams(dimension_semantics=sem, vmem_limit_bytes=VMEM_LIMIT)


def _rms_rows(x):
    return x * lax.rsqrt(jnp.mean(x * x, axis=-1, keepdims=True) + EPS)


def _gelu(x):
    return 0.5 * x * (1.0 + lax.erf(x * INV_SQRT2))


def _log2_sigmoid_pair(z2):
    ls = jnp.minimum(z2, 0.0) - jnp.log2(1.0 + jnp.exp2(-jnp.abs(z2)))
    return ls, ls - z2


def _split_bf16(x):
    hi = x.astype(BF16)
    lo = (x - hi.astype(F32)).astype(BF16)
    return hi, lo


def _inproj_kernel(x_ref, gmix_ref, w_ref, gq_ref, gk_ref, lng_ref, lnb_ref, ws_ref, bs_ref, gcm_ref,
                   q_ref, kf_ref, kb_ref, vf_ref, vb_ref, ocm_ref, *rest, n_heads, fresh_chunk_rows):
    cs_ref = rest[0] if fresh_chunk_rows else None
    xn_ref, u_ref = rest[-2:]
    j = pl.program_id(1)

    @pl.when(j == 0)
    def _():
        xn_ref[...] = (_rms_rows(x_ref[...]) * gmix_ref[...]).astype(BF16)

    tm = xn_ref.shape[0]
    heads_per_dot = 4

    def head_pairs():
        slab = heads_per_dot * HEAD_DIM
        for p in range(n_heads // heads_per_dot):
            zp = jnp.dot(xn_ref[...], w_ref[:, p * slab:(p + 1) * slab], preferred_element_type=F32)
            for hh in range(heads_per_dot):
                yield heads_per_dot * p + hh, zp[:, hh * HEAD_DIM:(hh + 1) * HEAD_DIM]

    def cols(h):
        return slice(h * HEAD_DIM, (h + 1) * HEAD_DIM)

    @pl.when(j == 0)
    def _():
        for h, z in head_pairs():
            q = _rms_rows(z) * gq_ref[...]
            q_ref[:, cols(h)] = (q * Q_SCALE).astype(q_ref.dtype)

    @pl.when(j == 1)
    def _():
        for h, z in head_pairs():
            k = _rms_rows(z) * gk_ref[...]
            kf_ref[:, cols(h)] = k
            kb_ref[:, cols(h)] = k.astype(BF16)

    @pl.when(j == 2)
    def _():
        for h, z in head_pairs():
            vf_ref[:, cols(h)] = z
            vb_ref[:, cols(h)] = z.astype(BF16)

    @pl.when(j == 3)
    def _():
        for h, z in head_pairs():
            u_ref[:, cols(h)] = _gelu(z)

    @pl.when(j == 4)
    def _():
        if not fresh_chunk_rows:
            row = lax.broadcasted_iota(jnp.int32, (CHUNK, CHUNK), 0)
            col = lax.broadcasted_iota(jnp.int32, (CHUNK, CHUNK), 1)
            tril = col <= row
        for h, z in head_pairs():
            gh = _gelu(z)
            mu = jnp.mean(gh, axis=-1, keepdims=True)
            xc = gh - mu
            vc = xc * lax.rsqrt(jnp.mean(xc * xc, axis=-1, keepdims=True) + EPS)
            vc = vc * lng_ref[:, cols(h)] + lnb_ref[:, cols(h)]
            if fresh_chunk_rows:
                cs_ref[:, cols(h)] = vc
                mixed = ws_ref[h, 0:1, 0:1] * vc + bs_ref[0:1, h:h + 1]
                u_ref[:, cols(h)] = u_ref[:, cols(h)] * mixed
            else:
                wm = jnp.where(tril, ws_ref[h], 0.0).astype(BF16)
                for c in range(tm // CHUNK):
                    rows = slice(c * CHUNK, (c + 1) * CHUNK)
                    mixed = jnp.dot(wm, vc[rows].astype(BF16), preferred_element_type=F32)
                    mixed = mixed + bs_ref[:, h:h + 1]
                    u_ref[rows, cols(h)] = u_ref[rows, cols(h)] * mixed
        ocm_ref[...] = (_rms_rows(u_ref[...]) * gcm_ref[...]).astype(BF16)


def _in_proj(x, g_mix, w_in_b, g_q, g_k, ln_g, ln_b, w_s, b_s_t, g_out_cm, *, tm, fresh_chunk_rows):
    T, d_model = x.shape
    n_heads = w_s.shape[0]
    gw = n_heads * HEAD_DIM
    assert w_in_b.shape == (d_model, 5 * gw) and T % tm == 0
    assert fresh_chunk_rows or tm % CHUNK == 0
    row_blk = lambda i, j: (i, 0)
    const2 = lambda i, j: (0, 0)
    out_f32 = jax.ShapeDtypeStruct((T, gw), F32)
    out_b16 = jax.ShapeDtypeStruct((T, gw), BF16)
    rows = lambda: pl.BlockSpec((tm, gw), row_blk)
    out_shape = [out_f32 if fresh_chunk_rows else out_b16, out_f32, out_b16, out_f32, out_b16, out_b16]
    if fresh_chunk_rows:
        out_shape.append(out_f32)
    return pl.pallas_call(
        functools.partial(_inproj_kernel, n_heads=n_heads, fresh_chunk_rows=fresh_chunk_rows),
        grid=(T // tm, 5),
        in_specs=[
            pl.BlockSpec((tm, d_model), row_blk),
            pl.BlockSpec((1, d_model), const2),
            pl.BlockSpec((d_model, gw), lambda i, j: (0, j)),
            pl.BlockSpec((1, HEAD_DIM), const2),
            pl.BlockSpec((1, HEAD_DIM), const2),
            pl.BlockSpec((1, gw), const2),
            pl.BlockSpec((1, gw), const2),
            pl.BlockSpec((n_heads, CHUNK, CHUNK), lambda i, j: (0, 0, 0)),
            pl.BlockSpec((CHUNK, n_heads), const2),
            pl.BlockSpec((1, gw), const2),
        ],
        out_specs=[rows() for _ in out_shape],
        out_shape=out_shape,
        scratch_shapes=[pltpu.VMEM((tm, d_model), BF16), pltpu.VMEM((tm, gw), F32)],
        compiler_params=_cparams("arbitrary", "arbitrary"),
        name="in_proj",
    )(x, g_mix, w_in_b, g_q, g_k, ln_g, ln_b, w_s, b_s_t, g_out_cm)


def _sb_prompt_kernel(bias_ref, q_ref, k_ref, v_ref, o_ref, *, blk, heads_per_step):
    hg = pl.program_id(1)
    i = pl.program_id(2)
    row = lax.broadcasted_iota(jnp.int32, (blk, blk), 0)
    col = lax.broadcasted_iota(jnp.int32, (blk, blk), 1)
    later = jnp.where(row > col, 1.0, 0.0).astype(BF16)
    causal = col < row

    def block(j, runs, on_diagonal):
        start = pl.multiple_of(j * blk, blk)
        heads = range(heads_per_step)
        cols = [slice(g * HEAD_DIM, (g + 1) * HEAD_DIM) for g in heads]
        zs = [lax.dot_general(q_ref[0, :, cols[g]], k_ref[0, pl.ds(start, blk), cols[g]],
                              (((1,), (1,)), ((), ())), preferred_element_type=F32)
              + bias_ref[hg * heads_per_step + g] * LOG2E for g in heads]
        lss, lks = [], []
        for g in heads:
            ls, lk = _log2_sigmoid_pair(zs[g])
            lss.append(ls)
            lks.append(jnp.where(causal, lk, 0.0) if on_diagonal else lk)
        afters = [jnp.dot(lk.astype(BF16), later, preferred_element_type=F32) for lk in lks]
        weights = []
        for g in heads:
            a = jnp.exp2(lss[g] + afters[g] + runs[g])
            weights.append((jnp.where(causal, a, 0.0) if on_diagonal else a).astype(BF16))
        for g in heads:
            pv = jnp.dot(weights[g], v_ref[0, pl.ds(start, blk), cols[g]], preferred_element_type=F32)
            if on_diagonal:
                o_ref[0, :, cols[g]] = pv
            else:
                o_ref[0, :, cols[g]] += pv
        return tuple(runs[g] + jnp.sum(lks[g], axis=-1, keepdims=True) for g in heads)

    runs = block(i, (jnp.zeros((blk, 1), F32),) * heads_per_step, True)
    lax.fori_loop(0, i, lambda t, runs: block(i - 1 - t, runs, False), runs)


def _sb_prompt(q_b, k_b, v_b, sb_bias, *, blk, heads_per_step):
    B, L, sbw = q_b.shape
    n_heads = sbw // HEAD_DIM
    assert L % blk == 0 and n_heads % heads_per_step == 0
    gw = heads_per_step * HEAD_DIM
    qo_spec = lambda: pl.BlockSpec((1, blk, gw), lambda b, h, i, bias: (b, i, h))
    kv_spec = lambda: pl.BlockSpec((1, L, gw), lambda b, h, i, bias: (b, 0, h))
    return pl.pallas_call(
        functools.partial(_sb_prompt_kernel, blk=blk, heads_per_step=heads_per_step),
        grid_spec=pltpu.PrefetchScalarGridSpec(
            num_scalar_prefetch=1,
            grid=(B, n_heads // heads_per_step, L // blk),
            in_specs=[qo_spec(), kv_spec(), kv_spec()],
            out_specs=qo_spec(),
        ),
        out_shape=jax.ShapeDtypeStruct((B, L, sbw), F32),
        compiler_params=_cparams("arbitrary", "arbitrary", "arbitrary"),
        name="sb_prompt",
    )(sb_bias, q_b, k_b, v_b)


SUBLANES = 8


def _page_weights(qm, bias2, k_ref):
    page, n_heads, hd = k_ref.shape
    n = page * n_heads
    lane = lax.broadcasted_iota(jnp.int32, (n_heads, n), 1)
    sub = lax.broadcasted_iota(jnp.int32, (n_heads, n), 0)
    kp = k_ref[...].reshape(n, hd).astype(BF16)
    z = lax.dot_general(qm, kp, (((1,), (1,)), ((), ())), preferred_element_type=F32) + bias2
    ls, lk = _log2_sigmoid_pair(z)
    suffix = lk
    shift = n_heads
    while shift < n:
        ahead = pltpu.roll(suffix, n - shift, axis=1)
        suffix = suffix + jnp.where(lane < n - shift, ahead, 0.0)
        shift *= 2
    total = jnp.sum(jnp.where(lane == sub, suffix, 0.0), axis=1, keepdims=True)
    a = jnp.where(lane % n_heads == sub, jnp.exp2(ls + (suffix - lk)), 0.0)
    hi, lo = _split_bf16(a)
    return jnp.concatenate([hi, lo], axis=0), total


def _page_output(weights, v_ref):
    page, n_heads, hd = v_ref.shape
    o2 = jnp.dot(weights, v_ref[...].reshape(page * n_heads, hd).astype(BF16), preferred_element_type=F32)
    return o2[:n_heads] + o2[n_heads:]


def _decode_combine_kernel(pacc_ref, ptot_ref, o_ref):
    n_pages = pacc_ref.shape[0]

    def body(t, carry):
        run, acc = carry
        p = n_pages - 1 - t
        return run + ptot_ref[p], acc + jnp.exp2(run) * pacc_ref[p]

    zero = jnp.zeros(o_ref.shape[1:], F32)
    _, acc = lax.fori_loop(0, n_pages, body, (zero, zero))
    o_ref[0] = acc


def _decode_combine(pacc, ptot, n_seq, pages_per_seq):
    _, n_heads, hd = pacc.shape
    seq_pages = lambda: pl.BlockSpec((pages_per_seq, n_heads, hd), lambda b: (b, 0, 0))
    out = pl.pallas_call(
        _decode_combine_kernel,
        grid=(n_seq,),
        in_specs=[seq_pages(), seq_pages()],
        out_specs=pl.BlockSpec((1, n_heads, hd), lambda b: (b, 0, 0)),
        out_shape=jax.ShapeDtypeStruct((n_seq, n_heads, hd), F32),
        compiler_params=_cparams("arbitrary"),
        name="sb_decode_combine",
    )(pacc, ptot)
    return out.reshape(n_seq, n_heads * hd)


def _out_proj_kernel(osb_ref, ocm_ref, h_ref, gsb_ref, w_ref, gffn_ref, h1_ref, f_ref, o_scr):
    sbw = osb_ref.shape[1]
    o_scr[:, :sbw] = (_rms_rows(osb_ref[...]) * gsb_ref[...]).astype(BF16)
    o_scr[:, sbw:] = ocm_ref[...]
    h1 = h_ref[...] + jnp.dot(o_scr[...], w_ref[...], preferred_element_type=F32)
    h1_ref[...] = h1
    f_ref[...] = (_rms_rows(h1) * gffn_ref[...]).astype(BF16)


def _out_proj(o_sb, o_cm_n, h, g_out_sb, w_out_b, g_ffn, *, tm):
    T, d_model = h.shape
    sbw, cmw = o_sb.shape[1], o_cm_n.shape[1]
    assert T % tm == 0 and w_out_b.shape == (sbw + cmw, d_model)
    row_blk = lambda i: (i, 0)
    const2 = lambda i: (0, 0)
    return pl.pallas_call(
        _out_proj_kernel,
        grid=(T // tm,),
        in_specs=[pl.BlockSpec((tm, sbw), row_blk),
                  pl.BlockSpec((tm, cmw), row_blk),
                  pl.BlockSpec((tm, d_model), row_blk),
                  pl.BlockSpec((1, sbw), const2),
                  pl.BlockSpec((sbw + cmw, d_model), const2),
                  pl.BlockSpec((1, d_model), const2)],
        out_specs=[pl.BlockSpec((tm, d_model), row_blk), pl.BlockSpec((tm, d_model), row_blk)],
        out_shape=[jax.ShapeDtypeStruct((T, d_model), F32), jax.ShapeDtypeStruct((T, d_model), BF16)],
        scratch_shapes=[pltpu.VMEM((tm, sbw + cmw), BF16)],
        compiler_params=_cparams("arbitrary"),
        name="out_proj",
    )(o_sb, o_cm_n, h, g_out_sb, w_out_b, g_ffn)


def _ffn_kernel(*refs, side_pages, pages_per_seq, total_pages):
    if side_pages:
        pt_ref, f_ref, wgu_ref, wd_ref, q_ref, bias_ref = refs[:6]
        k_refs = refs[6:6 + side_pages]
        v_refs = refs[6 + side_pages:6 + 2 * side_pages]
        delta_ref, pacc_ref, ptot_ref = refs[6 + 2 * side_pages:]
    else:
        f_ref, wgu_ref, wd_ref, delta_ref = refs
    tf = wd_ref.shape[0]
    i, j = pl.program_id(0), pl.program_id(1)

    @pl.when(j == 0)
    def _():
        delta_ref[...] = jnp.zeros_like(delta_ref)

    gate_up = jnp.dot(f_ref[...], jnp.concatenate([wgu_ref[0], wgu_ref[1]], axis=1),
                      preferred_element_type=F32)
    gate, up = gate_up[:, :tf], gate_up[:, tf:]
    if side_pages:
        first = (i * pl.num_programs(1) + j) * side_pages
        bias2 = bias_ref[...] * LOG2E
        weights = []
        for r in range(side_pages):
            seq = jnp.minimum(first + r, total_pages - 1) // pages_per_seq
            w, total = _page_weights(q_ref[seq].astype(BF16), bias2, k_refs[r])
            ptot_ref[r] = jnp.broadcast_to(total, ptot_ref.shape[1:])
            weights.append(w)
    act = (gate * jax.nn.sigmoid(gate) * up).astype(BF16)
    delta_ref[...] += jnp.dot(act, wd_ref[...], preferred_element_type=F32)
    if side_pages:
        for r in range(side_pages):
            pacc_ref[r] = _page_output(weights[r], v_refs[r])


def _ffn(f, wgu_b, wd_b, *, tm, tf, side=None, side_pages=0):
    T, d_model = f.shape
    d_ff = wd_b.shape[0]
    assert T % tm == 0 and d_ff % tf == 0
    grid = (T // tm, d_ff // tf)
    in_specs = [pl.BlockSpec((tm, d_model), lambda i, j, *_: (i, 0),
                             pipeline_mode=pl.Buffered(1) if side is not None else None),
                pl.BlockSpec((2, d_model, tf), lambda i, j, *_: (0, 0, j)),
                pl.BlockSpec((tf, d_model), lambda i, j, *_: (j, 0))]
    out_specs = [pl.BlockSpec((tm, d_model), lambda i, j, *_: (i, 0))]
    out_shape = [jax.ShapeDtypeStruct((T, d_model), F32)]
    args = [f, wgu_b, wd_b]
    if side is None:
        return pl.pallas_call(
            functools.partial(_ffn_kernel, side_pages=0, pages_per_seq=0, total_pages=0),
            grid=grid, in_specs=in_specs, out_specs=out_specs, out_shape=out_shape,
            compiler_params=_cparams("arbitrary", "arbitrary"), name="ffn",
        )(*args)[0]

    q, sb_bias, cache_k, cache_v, layer, page_table = side
    n_seq, pages_per_seq = page_table.shape
    _, _, page, n_heads, hd = cache_k.shape
    n = page * n_heads
    total_pages = n_seq * pages_per_seq
    n_steps = grid[0] * grid[1]
    assert n_heads == SUBLANES and hd == HEAD_DIM and n & (n - 1) == 0
    assert n_steps * side_pages >= total_pages

    def page_spec(r):
        def index(i, j, pt):
            g = jnp.minimum((i * grid[1] + j) * side_pages + r, total_pages - 1)
            return (layer, pt[g // pages_per_seq, g % pages_per_seq], 0, 0, 0)
        return pl.BlockSpec((None, None, page, n_heads, hd), index)

    step_pages = lambda: pl.BlockSpec((side_pages, n_heads, hd), lambda i, j, pt: (i * grid[1] + j, 0, 0))
    part = jax.ShapeDtypeStruct((n_steps * side_pages, n_heads, hd), F32)
    delta, pacc, ptot = pl.pallas_call(
        functools.partial(_ffn_kernel, side_pages=side_pages, pages_per_seq=pages_per_seq,
                          total_pages=total_pages),
        grid_spec=pltpu.PrefetchScalarGridSpec(
            num_scalar_prefetch=1,
            grid=grid,
            in_specs=in_specs
                     + [pl.BlockSpec((n_seq, n_heads, hd), lambda i, j, pt: (0, 0, 0)),
                        pl.BlockSpec((n_heads, n), lambda i, j, pt: (0, 0))]
                     + [page_spec(r) for r in range(side_pages)] * 2,
            out_specs=out_specs + [step_pages(), step_pages()],
        ),
        out_shape=out_shape + [part, part],
        compiler_params=_cparams("arbitrary", "arbitrary"),
        name="ffn_with_decode_pages",
    )(page_table, *args, q, jnp.broadcast_to(sb_bias[:, None], (n_heads, n)),
      *([cache_k] * side_pages), *([cache_v] * side_pages))
    return delta, _decode_combine(pacc, ptot, n_seq, pages_per_seq)


def _ple_kernel(h1_ref, delta_ref, p_ref, gple_ref, wg_ref, wp_ref, o_ref):
    h2 = h1_ref[...] + delta_ref[...]
    hn = (_rms_rows(h2) * gple_ref[...]).astype(BF16)
    gate = jax.nn.sigmoid(jnp.dot(hn, wg_ref[...], preferred_element_type=F32))
    proj = jnp.dot(p_ref[...].astype(BF16), wp_ref[...], preferred_element_type=F32)
    o_ref[...] = h2 + gate * proj


def _ple(h1, delta, p, g_ple, wpg_b, wpp_b, *, tm):
    T, d_model = h1.shape
    ple_dim = p.shape[1]
    assert T % tm == 0
    row_blk = lambda i: (i, 0)
    const2 = lambda i: (0, 0)
    return pl.pallas_call(
        _ple_kernel,
        grid=(T // tm,),
        in_specs=[pl.BlockSpec((tm, d_model), row_blk),
                  pl.BlockSpec((tm, d_model), row_blk),
                  pl.BlockSpec((tm, ple_dim), row_blk),
                  pl.BlockSpec((1, d_model), const2),
                  pl.BlockSpec((d_model, d_model), const2),
                  pl.BlockSpec((ple_dim, d_model), const2)],
        out_specs=pl.BlockSpec((tm, d_model), row_blk),
        out_shape=jax.ShapeDtypeStruct((T, d_model), F32),
        compiler_params=_cparams("arbitrary"),
        name="ple",
    )(h1, delta, p, g_ple, wpg_b, wpp_b)


PROMPT_FFN_ROWS = 1024
FFN_COLS = 256


def kernel(x_prompt, x_sample, cache_k, cache_v, page_table, p_prompt, p_sample, g_mix, w_in, g_q, g_k, sb_bias, sgu_ln_g, sgu_ln_b, w_s, b_s, g_out_sb, g_out_cm, w_out, g_ffn, w_gate, w_up, w_down, g_ple, w_ple_gate, w_ple_proj):
    depth = w_in.shape[0]
    B, L, d_model = x_prompt.shape
    Bs, Ls, _ = x_sample.shape
    n_sb = cache_k.shape[3]
    n_cm = w_s.shape[1]
    d_ff = w_gate.shape[2]
    sbw, cmw = n_sb * HEAD_DIM, n_cm * HEAD_DIM
    assert Ls == 1 and n_sb == n_cm and cache_k.shape[4] == HEAD_DIM
    assert (page_table.shape[1] * cache_k.shape[2]) % CHUNK == 0 and L % CHUNK == 0
    ffn_steps = (B * L // PROMPT_FFN_ROWS) * (d_ff // FFN_COLS)
    side_pages = -(-page_table.size // ffn_steps)

    hp = x_prompt.reshape(B * L, d_model)
    hs = x_sample.reshape(Bs * Ls, d_model)
    row = lambda a: a.reshape(1, -1)
    kp_l, vp_l, ks_l, vs_l, cs_l = [], [], [], [], []
    for i in range(depth):
        w_in_b = w_in[i].astype(BF16)
        w_out_b = w_out[i].astype(BF16)
        wgu_b = jnp.stack([w_gate[i].astype(BF16), w_up[i].astype(BF16)])
        wd_b = w_down[i].astype(BF16)
        wpg_b, wpp_b = w_ple_gate[i].astype(BF16), w_ple_proj[i].astype(BF16)
        proj_args = (row(g_mix[i]), w_in_b, row(g_q[i]), row(g_k[i]), row(sgu_ln_g[i]), row(sgu_ln_b[i]),
                     w_s[i], b_s[i].T, row(g_out_cm[i]))
        out_args = (row(g_out_sb[i]), w_out_b, row(g_ffn[i]))
        ple_args = (row(g_ple[i]), wpg_b, wpp_b)

        q_b, k_f, k_b, v_f, v_b, ocm_n = _in_proj(hp, *proj_args, tm=512, fresh_chunk_rows=False)
        q_s, ks_f, _, vs_f, _, ocm_s, cs = _in_proj(hs, *proj_args, tm=Bs, fresh_chunk_rows=True)

        o_sb = _sb_prompt(q_b.reshape(B, L, sbw), k_b.reshape(B, L, sbw), v_b.reshape(B, L, sbw),
                          sb_bias[i], blk=256, heads_per_step=4)
        h1, f = _out_proj(o_sb.reshape(B * L, sbw), ocm_n, hp, *out_args, tm=256)
        delta, o_sb_s = _ffn(f, wgu_b, wd_b, tm=PROMPT_FFN_ROWS, tf=FFN_COLS, side_pages=side_pages,
                             side=(q_s.reshape(Bs, n_sb, HEAD_DIM), sb_bias[i], cache_k, cache_v, i, page_table))
        hp = _ple(h1, delta, p_prompt[i].reshape(B * L, -1), *ple_args, tm=256)
        kp_l.append(k_f.reshape(B, L, n_sb, HEAD_DIM))
        vp_l.append(v_f.reshape(B, L, n_sb, HEAD_DIM))

        h1_s, f_s = _out_proj(o_sb_s, ocm_s, hs, *out_args, tm=Bs)
        delta_s = _ffn(f_s, wgu_b, wd_b, tm=Bs, tf=FFN_COLS)
        hs = _ple(h1_s, delta_s, p_sample[i].reshape(Bs, -1), *ple_args, tm=Bs)
        ks_l.append(ks_f.reshape(Bs, Ls, n_sb, HEAD_DIM))
        vs_l.append(vs_f.reshape(Bs, Ls, n_sb, HEAD_DIM))
        cs_l.append(cs.reshape(Bs, Ls, n_cm, HEAD_DIM))

    return (hp.reshape(B, L, d_model), hs.reshape(Bs, Ls, d_model),
            jnp.stack(kp_l), jnp.stack(vp_l), jnp.stack(ks_l), jnp.stack(vs_l), jnp.stack(cs_l))
```

```python
import functools
import math

import jax
import jax.numpy as jnp
from jax import lax
from jax.experimental import pallas as pl
from jax.experimental.pallas import tpu as pltpu

HEAD_DIM = 128
CHUNK = 128
EPS = 1e-6
SB_SCALE = 1.0 / math.sqrt(HEAD_DIM)
INV_SQRT2 = 0.7071067811865476
LOG2E = 1.4426950408889634
Q_SCALE = SB_SCALE * LOG2E

LANES = 128
VMEM_LIMIT = 56 * 1024 * 1024

F32 = jnp.float32
BF16 = jnp.bfloat16


def _cparams(*sem):
    return pltpu.CompilerParams(dimension_semantics=sem, vmem_limit_bytes=VMEM_LIMIT)


def _rms_rows(x):
    return x * lax.rsqrt(jnp.mean(x * x, axis=-1, keepdims=True) + EPS)


def _gelu(x):
    return 0.5 * x * (1.0 + lax.erf(x * INV_SQRT2))


def _log2_sigmoid_pair(z2):
    ls = jnp.minimum(z2, 0.0) - jnp.log2(1.0 + jnp.exp2(-jnp.abs(z2)))
    return ls, ls - z2


def _split_bf16(x):
    hi = x.astype(BF16)
    lo = (x - hi.astype(F32)).astype(BF16)
    return hi, lo


def _inproj_kernel(x_ref, gmix_ref, w_ref, gq_ref, gk_ref, lng_ref, lnb_ref, ws_ref, bs_ref, gcm_ref,
                   q_ref, kf_ref, kb_ref, vf_ref, vb_ref, ocm_ref, *rest, n_heads, fresh_chunk_rows):
    cs_ref = rest[0] if fresh_chunk_rows else None
    xn_ref, u_ref = rest[-2:]
    j = pl.program_id(1)

    @pl.when(j == 0)
    def _():
        xn_ref[...] = (_rms_rows(x_ref[...]) * gmix_ref[...]).astype(BF16)

    tm = xn_ref.shape[0]
    heads_per_dot = 4

    def head_pairs():
        slab = heads_per_dot * HEAD_DIM
        for p in range(n_heads // heads_per_dot):
            zp = jnp.dot(xn_ref[...], w_ref[:, p * slab:(p + 1) * slab], preferred_element_type=F32)
            for hh in range(heads_per_dot):
                yield heads_per_dot * p + hh, zp[:, hh * HEAD_DIM:(hh + 1) * HEAD_DIM]

    def cols(h):
        return slice(h * HEAD_DIM, (h + 1) * HEAD_DIM)

    @pl.when(j == 0)
    def _():
        for h, z in head_pairs():
            q = _rms_rows(z) * gq_ref[...]
            q_ref[:, cols(h)] = (q * Q_SCALE).astype(q_ref.dtype)

    @pl.when(j == 1)
    def _():
        for h, z in head_pairs():
            k = _rms_rows(z) * gk_ref[...]
            kf_ref[:, cols(h)] = k
            kb_ref[:, cols(h)] = k.astype(BF16)

    @pl.when(j == 2)
    def _():
        for h, z in head_pairs():
            vf_ref[:, cols(h)] = z
            vb_ref[:, cols(h)] = z.astype(BF16)

    @pl.when(j == 3)
    def _():
        for h, z in head_pairs():
            u_ref[:, cols(h)] = _gelu(z)

    @pl.when(j == 4)
    def _():
        if not fresh_chunk_rows:
            row = lax.broadcasted_iota(jnp.int32, (CHUNK, CHUNK), 0)
            col = lax.broadcasted_iota(jnp.int32, (CHUNK, CHUNK), 1)
            tril = col <= row
        for h, z in head_pairs():
            gh = _gelu(z)
            mu = jnp.mean(gh, axis=-1, keepdims=True)
            xc = gh - mu
            vc = xc * lax.rsqrt(jnp.mean(xc * xc, axis=-1, keepdims=True) + EPS)
            vc = vc * lng_ref[:, cols(h)] + lnb_ref[:, cols(h)]
            if fresh_chunk_rows:
                cs_ref[:, cols(h)] = vc
                mixed = ws_ref[h, 0:1, 0:1] * vc + bs_ref[0:1, h:h + 1]
                u_ref[:, cols(h)] = u_ref[:, cols(h)] * mixed
            else:
                wm = jnp.where(tril, ws_ref[h], 0.0).astype(BF16)
                for c in range(tm // CHUNK):
                    rows = slice(c * CHUNK, (c + 1) * CHUNK)
                    mixed = jnp.dot(wm, vc[rows].astype(BF16), preferred_element_type=F32)
                    mixed = mixed + bs_ref[:, h:h + 1]
                    u_ref[rows, cols(h)] = u_ref[rows, cols(h)] * mixed
        ocm_ref[...] = (_rms_rows(u_ref[...]) * gcm_ref[...]).astype(BF16)


def _in_proj(x, g_mix, w_in_b, g_q, g_k, ln_g, ln_b, w_s, b_s_t, g_out_cm, *, tm, fresh_chunk_rows):
    T, d_model = x.shape
    n_heads = w_s.shape[0]
    gw = n_heads * HEAD_DIM
    assert w_in_b.shape == (d_model, 5 * gw) and T % tm == 0
    assert fresh_chunk_rows or tm % CHUNK == 0
    row_blk = lambda i, j: (i, 0)
    const2 = lambda i, j: (0, 0)
    out_f32 = jax.ShapeDtypeStruct((T, gw), F32)
    out_b16 = jax.ShapeDtypeStruct((T, gw), BF16)
    rows = lambda: pl.BlockSpec((tm, gw), row_blk)
    out_shape = [out_f32 if fresh_chunk_rows else out_b16, out_f32, out_b16, out_f32, out_b16, out_b16]
    if fresh_chunk_rows:
        out_shape.append(out_f32)
    return pl.pallas_call(
        functools.partial(_inproj_kernel, n_heads=n_heads, fresh_chunk_rows=fresh_chunk_rows),
        grid=(T // tm, 5),
        in_specs=[
            pl.BlockSpec((tm, d_model), row_blk),
            pl.BlockSpec((1, d_model), const2),
            pl.BlockSpec((d_model, gw), lambda i, j: (0, j)),
            pl.BlockSpec((1, HEAD_DIM), const2),
            pl.BlockSpec((1, HEAD_DIM), const2),
            pl.BlockSpec((1, gw), const2),
            pl.BlockSpec((1, gw), const2),
            pl.BlockSpec((n_heads, CHUNK, CHUNK), lambda i, j: (0, 0, 0)),
            pl.BlockSpec((CHUNK, n_heads), const2),
            pl.BlockSpec((1, gw), const2),
        ],
        out_specs=[rows() for _ in out_shape],
        out_shape=out_shape,
        scratch_shapes=[pltpu.VMEM((tm, d_model), BF16), pltpu.VMEM((tm, gw), F32)],
        compiler_params=_cparams("arbitrary", "arbitrary"),
        name="in_proj",
    )(x, g_mix, w_in_b, g_q, g_k, ln_g, ln_b, w_s, b_s_t, g_out_cm)


def _sb_prompt_kernel(bias_ref, q_ref, k_ref, v_ref, o_ref, *, blk, heads_per_step):
    hg = pl.program_id(1)
    i = pl.program_id(2)
    row = lax.broadcasted_iota(jnp.int32, (blk, blk), 0)
    col = lax.broadcasted_iota(jnp.int32, (blk, blk), 1)
    later = jnp.where(row > col, 1.0, 0.0).astype(BF16)
    causal = col < row

    def block(j, runs, on_diagonal):
        start = pl.multiple_of(j * blk, blk)
        heads = range(heads_per_step)
        cols = [slice(g * HEAD_DIM, (g + 1) * HEAD_DIM) for g in heads]
        zs = [lax.dot_general(q_ref[0, :, cols[g]], k_ref[0, pl.ds(start, blk), cols[g]],
                              (((1,), (1,)), ((), ())), preferred_element_type=F32)
              + bias_ref[hg * heads_per_step + g] * LOG2E for g in heads]
        lss, lks = [], []
        for g in heads:
            ls, lk = _log2_sigmoid_pair(zs[g])
            lss.append(ls)
            lks.append(jnp.where(causal, lk, 0.0) if on_diagonal else lk)
        afters = [jnp.dot(lk.astype(BF16), later, preferred_element_type=F32) for lk in lks]
        weights = []
        for g in heads:
            a = jnp.exp2(lss[g] + afters[g] + runs[g])
            weights.append((jnp.where(causal, a, 0.0) if on_diagonal else a).astype(BF16))
        for g in heads:
            pv = jnp.dot(weights[g], v_ref[0, pl.ds(start, blk), cols[g]], preferred_element_type=F32)
            if on_diagonal:
                o_ref[0, :, cols[g]] = pv
            else:
                o_ref[0, :, cols[g]] += pv
        return tuple(runs[g] + jnp.sum(lks[g], axis=-1, keepdims=True) for g in heads)

    runs = block(i, (jnp.zeros((blk, 1), F32),) * heads_per_step, True)
    lax.fori_loop(0, i, lambda t, runs: block(i - 1 - t, runs, False), runs)


def _sb_prompt(q_b, k_b, v_b, sb_bias, *, blk, heads_per_step):
    B, L, sbw = q_b.shape
    n_heads = sbw // HEAD_DIM
    assert L % blk == 0 and n_heads % heads_per_step == 0
    gw = heads_per_step * HEAD_DIM
    qo_spec = lambda: pl.BlockSpec((1, blk, gw), lambda b, h, i, bias: (b, i, h))
    kv_spec = lambda: pl.BlockSpec((1, L, gw), lambda b, h, i, bias: (b, 0, h))
    return pl.pallas_call(
        functools.partial(_sb_prompt_kernel, blk=blk, heads_per_step=heads_per_step),
        grid_spec=pltpu.PrefetchScalarGridSpec(
            num_scalar_prefetch=1,
            grid=(B, n_heads // heads_per_step, L // blk),
            in_specs=[qo_spec(), kv_spec(), kv_spec()],
            out_specs=qo_spec(),
        ),
        out_shape=jax.ShapeDtypeStruct((B, L, sbw), F32),
        compiler_params=_cparams("arbitrary", "arbitrary", "arbitrary"),
        name="sb_prompt",
    )(sb_bias, q_b, k_b, v_b)


SUBLANES = 8


def _page_weights(qm, bias2, k_ref):
    page, n_heads, hd = k_ref.shape
    n = page * n_heads
    lane = lax.broadcasted_iota(jnp.int32, (n_heads, n), 1)
    sub = lax.broadcasted_iota(jnp.int32, (n_heads, n), 0)
    kp = k_ref[...].reshape(n, hd).astype(BF16)
    z = lax.dot_general(qm, kp, (((1,), (1,)), ((), ())), preferred_element_type=F32) + bias2
    ls, lk = _log2_sigmoid_pair(z)
    suffix = lk
    shift = n_heads
    while shift < n:
        ahead = pltpu.roll(suffix, n - shift, axis=1)
        suffix = suffix + jnp.where(lane < n - shift, ahead, 0.0)
        shift *= 2
    total = jnp.sum(jnp.where(lane == sub, suffix, 0.0), axis=1, keepdims=True)
    a = jnp.where(lane % n_heads == sub, jnp.exp2(ls + (suffix - lk)), 0.0)
    hi, lo = _split_bf16(a)
    return jnp.concatenate([hi, lo], axis=0), total


def _page_output(weights, v_ref):
    page, n_heads, hd = v_ref.shape
    o2 = jnp.dot(weights, v_ref[...].reshape(page * n_heads, hd).astype(BF16), preferred_element_type=F32)
    return o2[:n_heads] + o2[n_heads:]


COMBINE_SEQS = 8


def _decode_combine_kernel(pacc_ref, ptot_ref, o_ref, *, pages_per_seq):
    for s in range(o_ref.shape[0]):
        last = (s + 1) * pages_per_seq - 1

        def body(t, carry, last=last):
            run, acc = carry
            return run + ptot_ref[last - t], acc + jnp.exp2(run) * pacc_ref[last - t]

        zero = jnp.zeros(o_ref.shape[1:], F32)
        _, acc = lax.fori_loop(0, pages_per_seq, body, (zero, zero))
        o_ref[s] = acc


def _decode_combine(pacc, ptot, n_seq, pages_per_seq):
    _, n_heads, hd = pacc.shape
    group = math.gcd(n_seq, COMBINE_SEQS)
    seq_pages = lambda: pl.BlockSpec((group * pages_per_seq, n_heads, hd), lambda b: (b, 0, 0))
    out = pl.pallas_call(
        functools.partial(_decode_combine_kernel, pages_per_seq=pages_per_seq),
        grid=(n_seq // group,),
        in_specs=[seq_pages(), seq_pages()],
        out_specs=pl.BlockSpec((group, n_heads, hd), lambda b: (b, 0, 0)),
        out_shape=jax.ShapeDtypeStruct((n_seq, n_heads, hd), F32),
        compiler_params=_cparams("arbitrary"),
        name="sb_decode_combine",
    )(pacc, ptot)
    return out.reshape(n_seq, n_heads * hd)


def _cast_once(w_ref, w_scr):
    @pl.when(pl.program_id(0) == 0)
    def _():
        w_scr[...] = w_ref[...].astype(BF16)


def _resident(shape):
    return pl.BlockSpec(shape, lambda i: (0,) * len(shape), pipeline_mode=pl.Buffered(1))


def _out_proj_kernel(osb_ref, ocm_ref, h_ref, gsb_ref, w_ref, gffn_ref, h1_ref, f_ref, o_scr, w_scr):
    _cast_once(w_ref, w_scr)
    sbw = osb_ref.shape[1]
    o_scr[:, :sbw] = (_rms_rows(osb_ref[...]) * gsb_ref[...]).astype(BF16)
    o_scr[:, sbw:] = ocm_ref[...]
    h1 = h_ref[...] + jnp.dot(o_scr[...], w_scr[...], preferred_element_type=F32)
    h1_ref[...] = h1
    f_ref[...] = (_rms_rows(h1) * gffn_ref[...]).astype(BF16)


def _out_proj(o_sb, o_cm_n, h, g_out_sb, w_out, g_ffn, *, tm):
    T, d_model = h.shape
    sbw, cmw = o_sb.shape[1], o_cm_n.shape[1]
    assert T % tm == 0 and w_out.shape == (sbw + cmw, d_model)
    row_blk = lambda i: (i, 0)
    const2 = lambda i: (0, 0)
    return pl.pallas_call(
        _out_proj_kernel,
        grid=(T // tm,),
        in_specs=[pl.BlockSpec((tm, sbw), row_blk),
                  pl.BlockSpec((tm, cmw), row_blk),
                  pl.BlockSpec((tm, d_model), row_blk),
                  pl.BlockSpec((1, sbw), const2),
                  _resident((sbw + cmw, d_model)),
                  pl.BlockSpec((1, d_model), const2)],
        out_specs=[pl.BlockSpec((tm, d_model), row_blk), pl.BlockSpec((tm, d_model), row_blk)],
        out_shape=[jax.ShapeDtypeStruct((T, d_model), F32), jax.ShapeDtypeStruct((T, d_model), BF16)],
        scratch_shapes=[pltpu.VMEM((tm, sbw + cmw), BF16), pltpu.VMEM((sbw + cmw, d_model), BF16)],
        compiler_params=_cparams("arbitrary"),
        name="out_proj",
    )(o_sb, o_cm_n, h, g_out_sb, w_out, g_ffn)


def _ffn_kernel(*refs, side_pages, pages_per_seq, total_pages):
    if side_pages:
        slot_page_ref, f_ref, wg_ref, wu_ref, wd_ref, q_ref, bias_ref = refs[:7]
        k_refs = refs[7:7 + side_pages]
        v_refs = refs[7 + side_pages:7 + 2 * side_pages]
        delta_ref, pacc_ref, ptot_ref = refs[7 + 2 * side_pages:]
    else:
        f_ref, wg_ref, wu_ref, wd_ref, delta_ref = refs
    tf = wd_ref.shape[0]
    i, j = pl.program_id(0), pl.program_id(1)

    @pl.when(j == 0)
    def _():
        delta_ref[...] = jnp.zeros_like(delta_ref)

    gate_up = jnp.dot(f_ref[...], jnp.concatenate([wg_ref[...], wu_ref[...]], axis=1),
                      preferred_element_type=F32)
    gate, up = gate_up[:, :tf], gate_up[:, tf:]
    if side_pages:
        first = (i * pl.num_programs(1) + j) * side_pages
        bias2 = bias_ref[...] * LOG2E
        weights = []
        for r in range(side_pages):
            seq = jnp.minimum(first + r, total_pages - 1) // pages_per_seq
            w, total = _page_weights(q_ref[seq].astype(BF16), bias2, k_refs[r])
            ptot_ref[r] = jnp.broadcast_to(total, ptot_ref.shape[1:])
            weights.append(w)
    act = (gate * jax.nn.sigmoid(gate) * up).astype(BF16)
    delta_ref[...] += jnp.dot(act, wd_ref[...], preferred_element_type=F32)
    if side_pages:
        for r in range(side_pages):
            pacc_ref[r] = _page_output(weights[r], v_refs[r])


def _ffn(f, wg_b, wu_b, wd_b, *, tm, tf, side=None, side_pages=0):
    T, d_model = f.shape
    d_ff = wd_b.shape[0]
    assert T % tm == 0 and d_ff % tf == 0
    grid = (T // tm, d_ff // tf)
    in_specs = [pl.BlockSpec((tm, d_model), lambda i, j, *_: (i, 0),
                             pipeline_mode=pl.Buffered(1) if side is not None else None),
                pl.BlockSpec((d_model, tf), lambda i, j, *_: (0, j)),
                pl.BlockSpec((d_model, tf), lambda i, j, *_: (0, j)),
                pl.BlockSpec((tf, d_model), lambda i, j, *_: (j, 0))]
    out_specs = [pl.BlockSpec((tm, d_model), lambda i, j, *_: (i, 0))]
    out_shape = [jax.ShapeDtypeStruct((T, d_model), F32)]
    args = [f, wg_b, wu_b, wd_b]
    if side is None:
        return pl.pallas_call(
            functools.partial(_ffn_kernel, side_pages=0, pages_per_seq=0, total_pages=0),
            grid=grid, in_specs=in_specs, out_specs=out_specs, out_shape=out_shape,
            compiler_params=_cparams("arbitrary", "arbitrary"), name="ffn",
        )(*args)[0]

    q, sb_bias, cache_k, cache_v, layer, page_table = side
    n_seq, pages_per_seq = page_table.shape
    _, _, page, n_heads, hd = cache_k.shape
    n = page * n_heads
    total_pages = n_seq * pages_per_seq
    n_steps = grid[0] * grid[1]
    assert n_heads == SUBLANES and hd == HEAD_DIM and n & (n - 1) == 0
    assert n_steps * side_pages >= total_pages

    slot_page = jnp.pad(page_table.reshape(-1), (0, n_steps * side_pages - total_pages), mode="edge")

    def page_spec(r):
        return pl.BlockSpec((None, None, page, n_heads, hd),
                            lambda i, j, sp: (layer, sp[(i * grid[1] + j) * side_pages + r], 0, 0, 0))

    step_pages = lambda: pl.BlockSpec((side_pages, n_heads, hd), lambda i, j, pt: (i * grid[1] + j, 0, 0))
    part = jax.ShapeDtypeStruct((n_steps * side_pages, n_heads, hd), F32)
    delta, pacc, ptot = pl.pallas_call(
        functools.partial(_ffn_kernel, side_pages=side_pages, pages_per_seq=pages_per_seq,
                          total_pages=total_pages),
        grid_spec=pltpu.PrefetchScalarGridSpec(
            num_scalar_prefetch=1,
            grid=grid,
            in_specs=in_specs
                     + [pl.BlockSpec((n_seq, n_heads, hd), lambda i, j, pt: (0, 0, 0)),
                        pl.BlockSpec((n_heads, n), lambda i, j, pt: (0, 0))]
                     + [page_spec(r) for r in range(side_pages)] * 2,
            out_specs=out_specs + [step_pages(), step_pages()],
        ),
        out_shape=out_shape + [part, part],
        compiler_params=_cparams("arbitrary", "arbitrary"),
        name="ffn_with_decode_pages",
    )(slot_page, *args, q, jnp.broadcast_to(sb_bias[:, None], (n_heads, n)),
      *([cache_k] * side_pages), *([cache_v] * side_pages))
    return delta, _decode_combine(pacc, ptot, n_seq, pages_per_seq)


def _ple_kernel(h1_ref, delta_ref, p_ref, gple_ref, wg_ref, wp_ref, o_ref, wg_scr, wp_scr):
    _cast_once(wg_ref, wg_scr)
    _cast_once(wp_ref, wp_scr)
    h2 = h1_ref[...] + delta_ref[...]
    hn = (_rms_rows(h2) * gple_ref[...]).astype(BF16)
    gate = jax.nn.sigmoid(jnp.dot(hn, wg_scr[...], preferred_element_type=F32))
    proj = jnp.dot(p_ref[...].astype(BF16), wp_scr[...], preferred_element_type=F32)
    o_ref[...] = h2 + gate * proj


def _ple(h1, delta, p, g_ple, w_gate, w_proj, *, tm):
    T, d_model = h1.shape
    ple_dim = p.shape[1]
    assert T % tm == 0
    row_blk = lambda i: (i, 0)
    const2 = lambda i: (0, 0)
    return pl.pallas_call(
        _ple_kernel,
        grid=(T // tm,),
        in_specs=[pl.BlockSpec((tm, d_model), row_blk),
                  pl.BlockSpec((tm, d_model), row_blk),
                  pl.BlockSpec((tm, ple_dim), row_blk),
                  pl.BlockSpec((1, d_model), const2),
                  _resident((d_model, d_model)),
                  _resident((ple_dim, d_model))],
        out_specs=pl.BlockSpec((tm, d_model), row_blk),
        out_shape=jax.ShapeDtypeStruct((T, d_model), F32),
        scratch_shapes=[pltpu.VMEM((d_model, d_model), BF16), pltpu.VMEM((ple_dim, d_model), BF16)],
        compiler_params=_cparams("arbitrary"),
        name="ple",
    )(h1, delta, p, g_ple, w_gate, w_proj)


PROMPT_FFN_ROWS = 1024
FFN_COLS = 256


def kernel(x_prompt, x_sample, cache_k, cache_v, page_table, p_prompt, p_sample, g_mix, w_in, g_q, g_k, sb_bias, sgu_ln_g, sgu_ln_b, w_s, b_s, g_out_sb, g_out_cm, w_out, g_ffn, w_gate, w_up, w_down, g_ple, w_ple_gate, w_ple_proj):
    depth = w_in.shape[0]
    B, L, d_model = x_prompt.shape
    Bs, Ls, _ = x_sample.shape
    n_sb = cache_k.shape[3]
    n_cm = w_s.shape[1]
    d_ff = w_gate.shape[2]
    sbw, cmw = n_sb * HEAD_DIM, n_cm * HEAD_DIM
    assert Ls == 1 and n_sb == n_cm and cache_k.shape[4] == HEAD_DIM
    assert (page_table.shape[1] * cache_k.shape[2]) % CHUNK == 0 and L % CHUNK == 0
    ffn_steps = (B * L // PROMPT_FFN_ROWS) * (d_ff // FFN_COLS)
    side_pages = -(-page_table.size // ffn_steps)

    hp = x_prompt.reshape(B * L, d_model)
    hs = x_sample.reshape(Bs * Ls, d_model)
    row = lambda a: a.reshape(1, -1)
    kp_l, vp_l, ks_l, vs_l, cs_l = [], [], [], [], []
    for i in range(depth):
        w_in_b = w_in[i].astype(BF16)
        wg_b, wu_b = w_gate[i].astype(BF16), w_up[i].astype(BF16)
        wd_b = w_down[i].astype(BF16)
        proj_args = (row(g_mix[i]), w_in_b, row(g_q[i]), row(g_k[i]), row(sgu_ln_g[i]), row(sgu_ln_b[i]),
                     w_s[i], b_s[i].T, row(g_out_cm[i]))
        out_args = (row(g_out_sb[i]), w_out[i], row(g_ffn[i]))
        ple_args = (row(g_ple[i]), w_ple_gate[i], w_ple_proj[i])

        q_b, k_f, k_b, v_f, v_b, ocm_n = _in_proj(hp, *proj_args, tm=512, fresh_chunk_rows=False)
        q_s, ks_f, _, vs_f, _, ocm_s, cs = _in_proj(hs, *proj_args, tm=Bs, fresh_chunk_rows=True)

        o_sb = _sb_prompt(q_b.reshape(B, L, sbw), k_b.reshape(B, L, sbw), v_b.reshape(B, L, sbw),
                          sb_bias[i], blk=256, heads_per_step=4)
        h1, f = _out_proj(o_sb.reshape(B * L, sbw), ocm_n, hp, *out_args, tm=256)
        delta, o_sb_s = _ffn(f, wg_b, wu_b, wd_b, tm=PROMPT_FFN_ROWS, tf=FFN_COLS, side_pages=side_pages,
                             side=(q_s.reshape(Bs, n_sb, HEAD_DIM), sb_bias[i], cache_k, cache_v, i, page_table))
        hp = _ple(h1, delta, p_prompt[i].reshape(B * L, -1), *ple_args, tm=256)
        kp_l.append(k_f.reshape(B, L, n_sb, HEAD_DIM))
        vp_l.append(v_f.reshape(B, L, n_sb, HEAD_DIM))

        h1_s, f_s = _out_proj(o_sb_s, ocm_s, hs, *out_args, tm=Bs)
        delta_s = _ffn(f_s, wg_b, wu_b, wd_b, tm=Bs, tf=FFN_COLS)
        hs = _ple(h1_s, delta_s, p_sample[i].reshape(Bs, -1), *ple_args, tm=Bs)
        ks_l.append(ks_f.reshape(Bs, Ls, n_sb, HEAD_DIM))
        vs_l.append(vs_f.reshape(Bs, Ls, n_sb, HEAD_DIM))
        cs_l.append(cs.reshape(Bs, Ls, n_cm, HEAD_DIM))

    return (hp.reshape(B, L, d_model), hs.reshape(Bs, Ls, d_model),
            jnp.stack(kp_l), jnp.stack(vp_l), jnp.stack(ks_l), jnp.stack(vs_l), jnp.stack(cs_l))
```

```python
import functools
import math

import jax
import jax.numpy as jnp
from jax import lax
from jax.experimental import pallas as pl
from jax.experimental.pallas import tpu as pltpu

HEAD_DIM = 128
CHUNK = 128
EPS = 1e-6
SB_SCALE = 1.0 / math.sqrt(HEAD_DIM)
INV_SQRT2 = 0.7071067811865476
LOG2E = 1.4426950408889634
Q_SCALE = SB_SCALE * LOG2E

LANES = 128
VMEM_LIMIT = 56 * 1024 * 1024

F32 = jnp.float32
BF16 = jnp.bfloat16


def _cparams(*sem):
    return pltpu.CompilerParams(dimension_semantics=sem, vmem_limit_bytes=VMEM_LIMIT)


def _resident(shape):
    return pl.BlockSpec(shape, lambda i: (0,) * len(shape), pipeline_mode=pl.Buffered(1))


def _rms_rows(x):
    return x * lax.rsqrt(jnp.mean(x * x, axis=-1, keepdims=True) + EPS)


def _gelu(x):
    return 0.5 * x * (1.0 + lax.erf(x * INV_SQRT2))


def _log2_sigmoid_pair(z2):
    ls = jnp.minimum(z2, 0.0) - jnp.log2(1.0 + jnp.exp2(-jnp.abs(z2)))
    return ls, ls - z2


def _split_bf16(x):
    hi = x.astype(BF16)
    lo = (x - hi.astype(F32)).astype(BF16)
    return hi, lo


def _inproj_kernel(x_ref, gmix_ref, w_ref, gq_ref, gk_ref, lng_ref, lnb_ref, ws_ref, bs_ref, gcm_ref,
                   q_ref, kf_ref, kb_ref, vf_ref, vb_ref, ocm_ref, *rest, n_heads, fresh_chunk_rows):
    cs_ref = rest[0] if fresh_chunk_rows else None
    xn_ref, u_ref = rest[-2:]
    xn_ref[...] = (_rms_rows(x_ref[...]) * gmix_ref[...]).astype(BF16)

    tm = xn_ref.shape[0]
    heads_per_dot = 4

    def heads_of(group):
        slab = heads_per_dot * HEAD_DIM
        for p in range(n_heads // heads_per_dot):
            start = group * n_heads * HEAD_DIM + p * slab
            zp = jnp.dot(xn_ref[...], w_ref[:, start:start + slab], preferred_element_type=F32)
            for hh in range(heads_per_dot):
                yield heads_per_dot * p + hh, zp[:, hh * HEAD_DIM:(hh + 1) * HEAD_DIM]

    def cols(h):
        return slice(h * HEAD_DIM, (h + 1) * HEAD_DIM)

    for h, z in heads_of(0):
        q = _rms_rows(z) * gq_ref[...]
        q_ref[:, cols(h)] = (q * Q_SCALE).astype(q_ref.dtype)

    for h, z in heads_of(1):
        k = _rms_rows(z) * gk_ref[...]
        kf_ref[:, cols(h)] = k
        kb_ref[:, cols(h)] = k.astype(BF16)

    for h, z in heads_of(2):
        vf_ref[:, cols(h)] = z
        vb_ref[:, cols(h)] = z.astype(BF16)

    for h, z in heads_of(3):
        u_ref[:, cols(h)] = _gelu(z)

    if not fresh_chunk_rows:
        row = lax.broadcasted_iota(jnp.int32, (CHUNK, CHUNK), 0)
        col = lax.broadcasted_iota(jnp.int32, (CHUNK, CHUNK), 1)
        tril = col <= row
    for h, z in heads_of(4):
        gh = _gelu(z)
        mu = jnp.mean(gh, axis=-1, keepdims=True)
        xc = gh - mu
        vc = xc * lax.rsqrt(jnp.mean(xc * xc, axis=-1, keepdims=True) + EPS)
        vc = vc * lng_ref[:, cols(h)] + lnb_ref[:, cols(h)]
        if fresh_chunk_rows:
            cs_ref[:, cols(h)] = vc
            mixed = ws_ref[h, 0:1, 0:1] * vc + bs_ref[0:1, h:h + 1]
            u_ref[:, cols(h)] = u_ref[:, cols(h)] * mixed
        else:
            wm = jnp.where(tril, ws_ref[h], 0.0).astype(BF16)
            for c in range(tm // CHUNK):
                rows = slice(c * CHUNK, (c + 1) * CHUNK)
                mixed = jnp.dot(wm, vc[rows].astype(BF16), preferred_element_type=F32)
                mixed = mixed + bs_ref[:, h:h + 1]
                u_ref[rows, cols(h)] = u_ref[rows, cols(h)] * mixed
    ocm_ref[...] = (_rms_rows(u_ref[...]) * gcm_ref[...]).astype(BF16)


def _in_proj(x, g_mix, w_in_b, g_q, g_k, ln_g, ln_b, w_s, b_s_t, g_out_cm, *, tm, fresh_chunk_rows):
    T, d_model = x.shape
    n_heads = w_s.shape[0]
    gw = n_heads * HEAD_DIM
    assert w_in_b.shape == (d_model, 5 * gw) and T % tm == 0
    assert fresh_chunk_rows or tm % CHUNK == 0
    row_blk = lambda i: (i, 0)
    const2 = lambda i: (0, 0)
    out_f32 = jax.ShapeDtypeStruct((T, gw), F32)
    out_b16 = jax.ShapeDtypeStruct((T, gw), BF16)
    rows = lambda: pl.BlockSpec((tm, gw), row_blk)
    out_shape = [out_f32 if fresh_chunk_rows else out_b16, out_f32, out_b16, out_f32, out_b16, out_b16]
    if fresh_chunk_rows:
        out_shape.append(out_f32)
    return pl.pallas_call(
        functools.partial(_inproj_kernel, n_heads=n_heads, fresh_chunk_rows=fresh_chunk_rows),
        grid=(T // tm,),
        in_specs=[
            pl.BlockSpec((tm, d_model), row_blk),
            pl.BlockSpec((1, d_model), const2),
            _resident((d_model, 5 * gw)),
            pl.BlockSpec((1, HEAD_DIM), const2),
            pl.BlockSpec((1, HEAD_DIM), const2),
            pl.BlockSpec((1, gw), const2),
            pl.BlockSpec((1, gw), const2),
            pl.BlockSpec((n_heads, CHUNK, CHUNK), lambda i: (0, 0, 0)),
            pl.BlockSpec((CHUNK, n_heads), const2),
            pl.BlockSpec((1, gw), const2),
        ],
        out_specs=[rows() for _ in out_shape],
        out_shape=out_shape,
        scratch_shapes=[pltpu.VMEM((tm, d_model), BF16), pltpu.VMEM((tm, gw), F32)],
        compiler_params=_cparams("arbitrary"),
        name="in_proj",
    )(x, g_mix, w_in_b, g_q, g_k, ln_g, ln_b, w_s, b_s_t, g_out_cm)


def _sb_prompt_kernel(bias_ref, q_ref, k_ref, v_ref, o_ref, *, blk, heads_per_step):
    hg = pl.program_id(1)
    i = pl.program_id(2)
    row = lax.broadcasted_iota(jnp.int32, (blk, blk), 0)
    col = lax.broadcasted_iota(jnp.int32, (blk, blk), 1)
    later = jnp.where(row > col, 1.0, 0.0).astype(BF16)
    causal = col < row
    heads = range(heads_per_step)
    cols = [slice(g * HEAD_DIM, (g + 1) * HEAD_DIM) for g in heads]

    def block(j, runs, on_diagonal):
        start = pl.multiple_of(j * blk, blk)
        zs = [lax.dot_general(q_ref[0, :, cols[g]], k_ref[0, pl.ds(start, blk), cols[g]],
                              (((1,), (1,)), ((), ())), preferred_element_type=F32)
              + bias_ref[hg * heads_per_step + g] * LOG2E for g in heads]
        lss, lks = [], []
        for g in heads:
            ls, lk = _log2_sigmoid_pair(zs[g])
            lss.append(ls)
            lks.append(jnp.where(causal, lk, 0.0) if on_diagonal else lk)
        afters = [jnp.dot(lk.astype(BF16), later, preferred_element_type=F32) for lk in lks]
        weights = []
        for g in heads:
            a = jnp.exp2(lss[g] + afters[g] + runs[g])
            weights.append((jnp.where(causal, a, 0.0) if on_diagonal else a).astype(BF16))
        for g in heads:
            pv = jnp.dot(weights[g], v_ref[0, pl.ds(start, blk), cols[g]], preferred_element_type=F32)
            if on_diagonal:
                o_ref[0, :, cols[g]] = pv
            else:
                o_ref[0, :, cols[g]] += pv
        return tuple(runs[g] + jnp.sum(lks[g], axis=-1, keepdims=True) for g in heads)

    runs = block(i, (jnp.zeros((blk, 1), F32),) * heads_per_step, True)
    lax.fori_loop(0, i, lambda t, runs: block(i - 1 - t, runs, False), runs)


def _sb_prompt(q_b, k_b, v_b, sb_bias, *, blk, heads_per_step):
    B, L, sbw = q_b.shape
    n_heads = sbw // HEAD_DIM
    assert L % blk == 0 and n_heads % heads_per_step == 0
    gw = heads_per_step * HEAD_DIM
    qo_spec = lambda: pl.BlockSpec((1, blk, gw), lambda b, h, i, bias: (b, i, h))
    kv_spec = lambda: pl.BlockSpec((1, L, gw), lambda b, h, i, bias: (b, 0, h))
    return pl.pallas_call(
        functools.partial(_sb_prompt_kernel, blk=blk, heads_per_step=heads_per_step),
        grid_spec=pltpu.PrefetchScalarGridSpec(
            num_scalar_prefetch=1,
            grid=(B, n_heads // heads_per_step, L // blk),
            in_specs=[qo_spec(), kv_spec(), kv_spec()],
            out_specs=qo_spec(),
        ),
        out_shape=jax.ShapeDtypeStruct((B, L, sbw), F32),
        compiler_params=_cparams("arbitrary", "arbitrary", "arbitrary"),
        name="sb_prompt",
    )(sb_bias, q_b, k_b, v_b)


SUBLANES = 8


def _page_weights(qm, bias2, k_ref):
    page, n_heads, hd = k_ref.shape
    n = page * n_heads
    lane = lax.broadcasted_iota(jnp.int32, (n_heads, n), 1)
    sub = lax.broadcasted_iota(jnp.int32, (n_heads, n), 0)
    kp = k_ref[...].reshape(n, hd).astype(BF16)
    z = lax.dot_general(qm, kp, (((1,), (1,)), ((), ())), preferred_element_type=F32) + bias2
    ls, lk = _log2_sigmoid_pair(z)
    suffix = lk
    shift = n_heads
    while shift < n:
        ahead = pltpu.roll(suffix, n - shift, axis=1)
        suffix = suffix + jnp.where(lane < n - shift, ahead, 0.0)
        shift *= 2
    total = jnp.sum(jnp.where(lane == sub, suffix, 0.0), axis=1, keepdims=True)
    a = jnp.where(lane % n_heads == sub, jnp.exp2(ls + (suffix - lk)), 0.0)
    hi, lo = _split_bf16(a)
    return jnp.concatenate([hi, lo], axis=0), total


def _page_output(weights, v_ref):
    page, n_heads, hd = v_ref.shape
    o2 = jnp.dot(weights, v_ref[...].reshape(page * n_heads, hd).astype(BF16), preferred_element_type=F32)
    return o2[:n_heads] + o2[n_heads:]


COMBINE_SEQS = 8


def _decode_combine_kernel(pacc_ref, ptot_ref, o_ref, *, pages_per_seq):
    for s in range(o_ref.shape[0]):
        last = (s + 1) * pages_per_seq - 1

        def body(t, carry, last=last):
            run, acc = carry
            return run + ptot_ref[last - t], acc + jnp.exp2(run) * pacc_ref[last - t]

        zero = jnp.zeros(o_ref.shape[1:], F32)
        _, acc = lax.fori_loop(0, pages_per_seq, body, (zero, zero))
        o_ref[s] = acc


def _decode_combine(pacc, ptot, n_seq, pages_per_seq):
    _, n_heads, hd = pacc.shape
    group = math.gcd(n_seq, COMBINE_SEQS)
    seq_pages = lambda: pl.BlockSpec((group * pages_per_seq, n_heads, hd), lambda b: (b, 0, 0))
    out = pl.pallas_call(
        functools.partial(_decode_combine_kernel, pages_per_seq=pages_per_seq),
        grid=(n_seq // group,),
        in_specs=[seq_pages(), seq_pages()],
        out_specs=pl.BlockSpec((group, n_heads, hd), lambda b: (b, 0, 0)),
        out_shape=jax.ShapeDtypeStruct((n_seq, n_heads, hd), F32),
        compiler_params=_cparams("arbitrary"),
        name="sb_decode_combine",
    )(pacc, ptot)
    return out.reshape(n_seq, n_heads * hd)


def _cast_once(w_ref, w_scr):
    @pl.when(pl.program_id(0) == 0)
    def _():
        w_scr[...] = w_ref[...].astype(BF16)


def _out_proj_kernel(osb_ref, ocm_ref, h_ref, gsb_ref, w_ref, gffn_ref, h1_ref, f_ref, o_scr, w_scr):
    _cast_once(w_ref, w_scr)
    sbw = osb_ref.shape[1]
    o_scr[:, :sbw] = (_rms_rows(osb_ref[...]) * gsb_ref[...]).astype(BF16)
    o_scr[:, sbw:] = ocm_ref[...]
    h1 = h_ref[...] + jnp.dot(o_scr[...], w_scr[...], preferred_element_type=F32)
    h1_ref[...] = h1
    f_ref[...] = (_rms_rows(h1) * gffn_ref[...]).astype(BF16)


def _out_proj(o_sb, o_cm_n, h, g_out_sb, w_out, g_ffn, *, tm):
    T, d_model = h.shape
    sbw, cmw = o_sb.shape[1], o_cm_n.shape[1]
    assert T % tm == 0 and w_out.shape == (sbw + cmw, d_model)
    row_blk = lambda i: (i, 0)
    const2 = lambda i: (0, 0)
    return pl.pallas_call(
        _out_proj_kernel,
        grid=(T // tm,),
        in_specs=[pl.BlockSpec((tm, sbw), row_blk),
                  pl.BlockSpec((tm, cmw), row_blk),
                  pl.BlockSpec((tm, d_model), row_blk),
                  pl.BlockSpec((1, sbw), const2),
                  _resident((sbw + cmw, d_model)),
                  pl.BlockSpec((1, d_model), const2)],
        out_specs=[pl.BlockSpec((tm, d_model), row_blk), pl.BlockSpec((tm, d_model), row_blk)],
        out_shape=[jax.ShapeDtypeStruct((T, d_model), F32), jax.ShapeDtypeStruct((T, d_model), BF16)],
        scratch_shapes=[pltpu.VMEM((tm, sbw + cmw), BF16), pltpu.VMEM((sbw + cmw, d_model), BF16)],
        compiler_params=_cparams("arbitrary"),
        name="out_proj",
    )(o_sb, o_cm_n, h, g_out_sb, w_out, g_ffn)


def _ffn_kernel(*refs, side_pages, pages_per_seq, total_pages):
    if side_pages:
        slot_page_ref, f_ref, wg_ref, wu_ref, wd_ref, q_ref, bias_ref = refs[:7]
        k_refs = refs[7:7 + side_pages]
        v_refs = refs[7 + side_pages:7 + 2 * side_pages]
        delta_ref, pacc_ref, ptot_ref = refs[7 + 2 * side_pages:]
    else:
        f_ref, wg_ref, wu_ref, wd_ref, delta_ref = refs
    tf = wd_ref.shape[0]
    i, j = pl.program_id(0), pl.program_id(1)

    @pl.when(j == 0)
    def _():
        delta_ref[...] = jnp.zeros_like(delta_ref)

    gate_up = jnp.dot(f_ref[...], jnp.concatenate([wg_ref[...], wu_ref[...]], axis=1),
                      preferred_element_type=F32)
    gate, up = gate_up[:, :tf], gate_up[:, tf:]
    if side_pages:
        first = (i * pl.num_programs(1) + j) * side_pages
        bias2 = bias_ref[...] * LOG2E
        weights = []
        for r in range(side_pages):
            seq = jnp.minimum(first + r, total_pages - 1) // pages_per_seq
            w, total = _page_weights(q_ref[seq].astype(BF16), bias2, k_refs[r])
            ptot_ref[r] = jnp.broadcast_to(total, ptot_ref.shape[1:])
            weights.append(w)
    act = (gate * jax.nn.sigmoid(gate) * up).astype(BF16)
    delta_ref[...] += jnp.dot(act, wd_ref[...], preferred_element_type=F32)
    if side_pages:
        for r in range(side_pages):
            pacc_ref[r] = _page_output(weights[r], v_refs[r])


def _ffn(f, wg_b, wu_b, wd_b, *, tm, tf, side=None, side_pages=0):
    T, d_model = f.shape
    d_ff = wd_b.shape[0]
    assert T % tm == 0 and d_ff % tf == 0
    grid = (T // tm, d_ff // tf)
    in_specs = [pl.BlockSpec((tm, d_model), lambda i, j, *_: (i, 0),
                             pipeline_mode=pl.Buffered(1) if side is not None else None),
                pl.BlockSpec((d_model, tf), lambda i, j, *_: (0, j)),
                pl.BlockSpec((d_model, tf), lambda i, j, *_: (0, j)),
                pl.BlockSpec((tf, d_model), lambda i, j, *_: (j, 0))]
    out_specs = [pl.BlockSpec((tm, d_model), lambda i, j, *_: (i, 0))]
    out_shape = [jax.ShapeDtypeStruct((T, d_model), F32)]
    args = [f, wg_b, wu_b, wd_b]
    if side is None:
        return pl.pallas_call(
            functools.partial(_ffn_kernel, side_pages=0, pages_per_seq=0, total_pages=0),
            grid=grid, in_specs=in_specs, out_specs=out_specs, out_shape=out_shape,
            compiler_params=_cparams("arbitrary", "arbitrary"), name="ffn",
        )(*args)[0]

    q, sb_bias, cache_k, cache_v, layer, page_table = side
    n_seq, pages_per_seq = page_table.shape
    _, _, page, n_heads, hd = cache_k.shape
    n = page * n_heads
    total_pages = n_seq * pages_per_seq
    n_steps = grid[0] * grid[1]
    assert n_heads == SUBLANES and hd == HEAD_DIM and n & (n - 1) == 0
    assert n_steps * side_pages >= total_pages

    slot_page = jnp.pad(page_table.reshape(-1), (0, n_steps * side_pages - total_pages), mode="edge")

    def page_spec(r):
        return pl.BlockSpec((None, None, page, n_heads, hd),
                            lambda i, j, sp: (layer, sp[(i * grid[1] + j) * side_pages + r], 0, 0, 0))

    step_pages = lambda: pl.BlockSpec((side_pages, n_heads, hd), lambda i, j, pt: (i * grid[1] + j, 0, 0))
    part = jax.ShapeDtypeStruct((n_steps * side_pages, n_heads, hd), F32)
    delta, pacc, ptot = pl.pallas_call(
        functools.partial(_ffn_kernel, side_pages=side_pages, pages_per_seq=pages_per_seq,
                          total_pages=total_pages),
        grid_spec=pltpu.PrefetchScalarGridSpec(
            num_scalar_prefetch=1,
            grid=grid,
            in_specs=in_specs
                     + [pl.BlockSpec((n_seq, n_heads, hd), lambda i, j, pt: (0, 0, 0)),
                        pl.BlockSpec((n_heads, n), lambda i, j, pt: (0, 0))]
                     + [page_spec(r) for r in range(side_pages)] * 2,
            out_specs=out_specs + [step_pages(), step_pages()],
        ),
        out_shape=out_shape + [part, part],
        compiler_params=_cparams("arbitrary", "arbitrary"),
        name="ffn_with_decode_pages",
    )(slot_page, *args, q, jnp.broadcast_to(sb_bias[:, None], (n_heads, n)),
      *([cache_k] * side_pages), *([cache_v] * side_pages))
    return delta, _decode_combine(pacc, ptot, n_seq, pages_per_seq)


def _ple_kernel(h1_ref, delta_ref, p_ref, gple_ref, wg_ref, wp_ref, o_ref, wg_scr, wp_scr):
    _cast_once(wg_ref, wg_scr)
    _cast_once(wp_ref, wp_scr)
    h2 = h1_ref[...] + delta_ref[...]
    hn = (_rms_rows(h2) * gple_ref[...]).astype(BF16)
    gate = jax.nn.sigmoid(jnp.dot(hn, wg_scr[...], preferred_element_type=F32))
    proj = jnp.dot(p_ref[...].astype(BF16), wp_scr[...], preferred_element_type=F32)
    o_ref[...] = h2 + gate * proj


def _ple(h1, delta, p, g_ple, w_gate, w_proj, *, tm):
    T, d_model = h1.shape
    ple_dim = p.shape[1]
    assert T % tm == 0
    row_blk = lambda i: (i, 0)
    const2 = lambda i: (0, 0)
    return pl.pallas_call(
        _ple_kernel,
        grid=(T // tm,),
        in_specs=[pl.BlockSpec((tm, d_model), row_blk),
                  pl.BlockSpec((tm, d_model), row_blk),
                  pl.BlockSpec((tm, ple_dim), row_blk),
                  pl.BlockSpec((1, d_model), const2),
                  _resident((d_model, d_model)),
                  _resident((ple_dim, d_model))],
        out_specs=pl.BlockSpec((tm, d_model), row_blk),
        out_shape=jax.ShapeDtypeStruct((T, d_model), F32),
        scratch_shapes=[pltpu.VMEM((d_model, d_model), BF16), pltpu.VMEM((ple_dim, d_model), BF16)],
        compiler_params=_cparams("arbitrary"),
        name="ple",
    )(h1, delta, p, g_ple, w_gate, w_proj)


PROMPT_FFN_ROWS = 1024
FFN_COLS = 256


def kernel(x_prompt, x_sample, cache_k, cache_v, page_table, p_prompt, p_sample, g_mix, w_in, g_q, g_k, sb_bias, sgu_ln_g, sgu_ln_b, w_s, b_s, g_out_sb, g_out_cm, w_out, g_ffn, w_gate, w_up, w_down, g_ple, w_ple_gate, w_ple_proj):
    depth = w_in.shape[0]
    B, L, d_model = x_prompt.shape
    Bs, Ls, _ = x_sample.shape
    n_sb = cache_k.shape[3]
    n_cm = w_s.shape[1]
    d_ff = w_gate.shape[2]
    sbw, cmw = n_sb * HEAD_DIM, n_cm * HEAD_DIM
    assert Ls == 1 and n_sb == n_cm and cache_k.shape[4] == HEAD_DIM
    assert (page_table.shape[1] * cache_k.shape[2]) % CHUNK == 0 and L % CHUNK == 0
    ffn_steps = (B * L // PROMPT_FFN_ROWS) * (d_ff // FFN_COLS)
    side_pages = -(-page_table.size // ffn_steps)

    hp = x_prompt.reshape(B * L, d_model)
    hs = x_sample.reshape(Bs * Ls, d_model)
    row = lambda a: a.reshape(1, -1)
    kp_l, vp_l, ks_l, vs_l, cs_l = [], [], [], [], []
    for i in range(depth):
        w_in_b = w_in[i].astype(BF16)
        wg_b, wu_b = w_gate[i].astype(BF16), w_up[i].astype(BF16)
        wd_b = w_down[i].astype(BF16)
        proj_args = (row(g_mix[i]), w_in_b, row(g_q[i]), row(g_k[i]), row(sgu_ln_g[i]), row(sgu_ln_b[i]),
                     w_s[i], b_s[i].T, row(g_out_cm[i]))
        out_args = (row(g_out_sb[i]), w_out[i], row(g_ffn[i]))
        ple_args = (row(g_ple[i]), w_ple_gate[i], w_ple_proj[i])

        q_b, k_f, k_b, v_f, v_b, ocm_n = _in_proj(hp, *proj_args, tm=512, fresh_chunk_rows=False)
        q_s, ks_f, _, vs_f, _, ocm_s, cs = _in_proj(hs, *proj_args, tm=Bs, fresh_chunk_rows=True)

        o_sb = _sb_prompt(q_b.reshape(B, L, sbw), k_b.reshape(B, L, sbw), v_b.reshape(B, L, sbw),
                          sb_bias[i], blk=256, heads_per_step=4)
        h1, f = _out_proj(o_sb.reshape(B * L, sbw), ocm_n, hp, *out_args, tm=256)
        delta, o_sb_s = _ffn(f, wg_b, wu_b, wd_b, tm=PROMPT_FFN_ROWS, tf=FFN_COLS, side_pages=side_pages,
                             side=(q_s.reshape(Bs, n_sb, HEAD_DIM), sb_bias[i], cache_k, cache_v, i, page_table))
        hp = _ple(h1, delta, p_prompt[i].reshape(B * L, -1), *ple_args, tm=256)
        kp_l.append(k_f.reshape(B, L, n_sb, HEAD_DIM))
        vp_l.append(v_f.reshape(B, L, n_sb, HEAD_DIM))

        h1_s, f_s = _out_proj(o_sb_s, ocm_s, hs, *out_args, tm=Bs)
        delta_s = _ffn(f_s, wg_b, wu_b, wd_b, tm=Bs, tf=FFN_COLS)
        hs = _ple(h1_s, delta_s, p_sample[i].reshape(Bs, -1), *ple_args, tm=Bs)
        ks_l.append(ks_f.reshape(Bs, Ls, n_sb, HEAD_DIM))
        vs_l.append(vs_f.reshape(Bs, Ls, n_sb, HEAD_DIM))
        cs_l.append(cs.reshape(Bs, Ls, n_cm, HEAD_DIM))

    return (hp.reshape(B, L, d_model), hs.reshape(Bs, Ls, d_model),
            jnp.stack(kp_l), jnp.stack(vp_l), jnp.stack(ks_l), jnp.stack(vs_l), jnp.stack(cs_l))
```

```python
import functools
import math

import jax
import jax.numpy as jnp
from jax import lax
from jax.experimental import pallas as pl
from jax.experimental.pallas import tpu as pltpu

HEAD_DIM = 128
CHUNK = 128
EPS = 1e-6
SB_SCALE = 1.0 / math.sqrt(HEAD_DIM)
INV_SQRT2 = 0.7071067811865476
LOG2E = 1.4426950408889634
Q_SCALE = SB_SCALE * LOG2E

LANES = 128
VMEM_LIMIT = 56 * 1024 * 1024

F32 = jnp.float32
BF16 = jnp.bfloat16


def _cparams(*sem):
    return pltpu.CompilerParams(dimension_semantics=sem, vmem_limit_bytes=VMEM_LIMIT)


def _resident(shape):
    return pl.BlockSpec(shape, lambda i: (0,) * len(shape), pipeline_mode=pl.Buffered(1))


def _rms_rows(x):
    return x * lax.rsqrt(jnp.mean(x * x, axis=-1, keepdims=True) + EPS)


def _gelu(x):
    return 0.5 * x * (1.0 + lax.erf(x * INV_SQRT2))


def _log2_sigmoid_pair(z2):
    ls = jnp.minimum(z2, 0.0) - jnp.log2(1.0 + jnp.exp2(-jnp.abs(z2)))
    return ls, ls - z2


def _split_bf16(x):
    hi = x.astype(BF16)
    lo = (x - hi.astype(F32)).astype(BF16)
    return hi, lo


def _inproj_kernel(x_ref, gmix_ref, w_ref, gq_ref, gk_ref, lng_ref, lnb_ref, ws_ref, bs_ref, gcm_ref,
                   q_ref, kf_ref, kb_ref, vf_ref, vb_ref, ocm_ref, *rest, n_heads, fresh_chunk_rows):
    cs_ref = rest[0] if fresh_chunk_rows else None
    xn_ref, u_ref = rest[-2:]
    xn_ref[...] = (_rms_rows(x_ref[...]) * gmix_ref[...]).astype(BF16)

    tm = xn_ref.shape[0]
    heads_per_dot = 4

    def heads_of(group):
        slab = heads_per_dot * HEAD_DIM
        for p in range(n_heads // heads_per_dot):
            start = group * n_heads * HEAD_DIM + p * slab
            zp = jnp.dot(xn_ref[...], w_ref[:, start:start + slab], preferred_element_type=F32)
            for hh in range(heads_per_dot):
                yield heads_per_dot * p + hh, zp[:, hh * HEAD_DIM:(hh + 1) * HEAD_DIM]

    def cols(h):
        return slice(h * HEAD_DIM, (h + 1) * HEAD_DIM)

    for h, z in heads_of(0):
        q = _rms_rows(z) * gq_ref[...]
        q_ref[:, cols(h)] = (q * Q_SCALE).astype(q_ref.dtype)

    for h, z in heads_of(1):
        k = _rms_rows(z) * gk_ref[...]
        kf_ref[:, cols(h)] = k
        kb_ref[:, cols(h)] = k.astype(BF16)

    for h, z in heads_of(2):
        vf_ref[:, cols(h)] = z
        vb_ref[:, cols(h)] = z.astype(BF16)

    for h, z in heads_of(3):
        u_ref[:, cols(h)] = _gelu(z)

    if not fresh_chunk_rows:
        row = lax.broadcasted_iota(jnp.int32, (CHUNK, CHUNK), 0)
        col = lax.broadcasted_iota(jnp.int32, (CHUNK, CHUNK), 1)
        tril = col <= row
    for h, z in heads_of(4):
        gh = _gelu(z)
        mu = jnp.mean(gh, axis=-1, keepdims=True)
        xc = gh - mu
        vc = xc * lax.rsqrt(jnp.mean(xc * xc, axis=-1, keepdims=True) + EPS)
        vc = vc * lng_ref[:, cols(h)] + lnb_ref[:, cols(h)]
        if fresh_chunk_rows:
            cs_ref[:, cols(h)] = vc
            mixed = ws_ref[h, 0:1, 0:1] * vc + bs_ref[0:1, h:h + 1]
            u_ref[:, cols(h)] = u_ref[:, cols(h)] * mixed
        else:
            wm = jnp.where(tril, ws_ref[h], 0.0).astype(BF16)
            for c in range(tm // CHUNK):
                rows = slice(c * CHUNK, (c + 1) * CHUNK)
                mixed = jnp.dot(wm, vc[rows].astype(BF16), preferred_element_type=F32)
                mixed = mixed + bs_ref[:, h:h + 1]
                u_ref[rows, cols(h)] = u_ref[rows, cols(h)] * mixed
    ocm_ref[...] = (_rms_rows(u_ref[...]) * gcm_ref[...]).astype(BF16)


def _in_proj(x, g_mix, w_in_b, g_q, g_k, ln_g, ln_b, w_s, b_s_t, g_out_cm, *, tm, fresh_chunk_rows):
    T, d_model = x.shape
    n_heads = w_s.shape[0]
    gw = n_heads * HEAD_DIM
    assert w_in_b.shape == (d_model, 5 * gw) and T % tm == 0
    assert fresh_chunk_rows or tm % CHUNK == 0
    row_blk = lambda i: (i, 0)
    const2 = lambda i: (0, 0)
    out_f32 = jax.ShapeDtypeStruct((T, gw), F32)
    out_b16 = jax.ShapeDtypeStruct((T, gw), BF16)
    rows = lambda: pl.BlockSpec((tm, gw), row_blk)
    out_shape = [out_f32 if fresh_chunk_rows else out_b16, out_f32, out_b16, out_f32, out_b16, out_b16]
    if fresh_chunk_rows:
        out_shape.append(out_f32)
    return pl.pallas_call(
        functools.partial(_inproj_kernel, n_heads=n_heads, fresh_chunk_rows=fresh_chunk_rows),
        grid=(T // tm,),
        in_specs=[
            pl.BlockSpec((tm, d_model), row_blk),
            pl.BlockSpec((1, d_model), const2),
            _resident((d_model, 5 * gw)),
            pl.BlockSpec((1, HEAD_DIM), const2),
            pl.BlockSpec((1, HEAD_DIM), const2),
            pl.BlockSpec((1, gw), const2),
            pl.BlockSpec((1, gw), const2),
            pl.BlockSpec((n_heads, CHUNK, CHUNK), lambda i: (0, 0, 0)),
            pl.BlockSpec((CHUNK, n_heads), const2),
            pl.BlockSpec((1, gw), const2),
        ],
        out_specs=[rows() for _ in out_shape],
        out_shape=out_shape,
        scratch_shapes=[pltpu.VMEM((tm, d_model), BF16), pltpu.VMEM((tm, gw), F32)],
        compiler_params=_cparams("arbitrary"),
        name="in_proj",
    )(x, g_mix, w_in_b, g_q, g_k, ln_g, ln_b, w_s, b_s_t, g_out_cm)


def _sb_prompt_kernel(bias_ref, q_ref, k_ref, v_ref, *rest, blk, heads_per_step):
    n_cast = (len(rest) - 1) // 2
    o_ref = rest[n_cast]
    for w_ref, wb_ref in zip(rest[:n_cast], rest[n_cast + 1:]):
        wb_ref[...] = w_ref[...].astype(BF16)
    hg = pl.program_id(1)
    i = pl.program_id(2)
    row = lax.broadcasted_iota(jnp.int32, (blk, blk), 0)
    col = lax.broadcasted_iota(jnp.int32, (blk, blk), 1)
    later = jnp.where(row > col, 1.0, 0.0).astype(BF16)
    causal = col < row
    heads = range(heads_per_step)
    cols = [slice(g * HEAD_DIM, (g + 1) * HEAD_DIM) for g in heads]

    def block(j, runs, on_diagonal):
        start = pl.multiple_of(j * blk, blk)
        zs = [lax.dot_general(q_ref[0, :, cols[g]], k_ref[0, pl.ds(start, blk), cols[g]],
                              (((1,), (1,)), ((), ())), preferred_element_type=F32)
              + bias_ref[hg * heads_per_step + g] * LOG2E for g in heads]
        lss, lks = [], []
        for g in heads:
            ls, lk = _log2_sigmoid_pair(zs[g])
            lss.append(ls)
            lks.append(jnp.where(causal, lk, 0.0) if on_diagonal else lk)
        afters = [jnp.dot(lk.astype(BF16), later, preferred_element_type=F32) for lk in lks]
        weights = []
        for g in heads:
            a = jnp.exp2(lss[g] + afters[g] + runs[g])
            weights.append((jnp.where(causal, a, 0.0) if on_diagonal else a).astype(BF16))
        for g in heads:
            pv = jnp.dot(weights[g], v_ref[0, pl.ds(start, blk), cols[g]], preferred_element_type=F32)
            if on_diagonal:
                o_ref[0, :, cols[g]] = pv
            else:
                o_ref[0, :, cols[g]] += pv
        return tuple(runs[g] + jnp.sum(lks[g], axis=-1, keepdims=True) for g in heads)

    runs = block(i, (jnp.zeros((blk, 1), F32),) * heads_per_step, True)
    lax.fori_loop(0, i, lambda t, runs: block(i - 1 - t, runs, False), runs)


BF16_SUBLANES = 16


def _sb_prompt(q_b, k_b, v_b, sb_bias, to_cast=(), *, blk, heads_per_step):
    B, L, sbw = q_b.shape
    n_heads = sbw // HEAD_DIM
    assert L % blk == 0 and n_heads % heads_per_step == 0
    gw = heads_per_step * HEAD_DIM
    grid = (B, n_heads // heads_per_step, L // blk)
    n_steps = grid[0] * grid[1] * grid[2]
    qo_spec = lambda: pl.BlockSpec((1, blk, gw), lambda b, h, i, bias: (b, i, h))
    kv_spec = lambda: pl.BlockSpec((1, L, gw), lambda b, h, i, bias: (b, 0, h))

    def slab_spec(w):
        rows = next(r for r in range(BF16_SUBLANES, w.shape[0] + 1, BF16_SUBLANES)
                    if w.shape[0] % r == 0 and r * n_steps >= w.shape[0])
        last = w.shape[0] // rows - 1
        return pl.BlockSpec(
            (rows, w.shape[1]),
            lambda b, h, i, bias: (jnp.minimum((b * grid[1] + h) * grid[2] + i, last), 0))

    out = pl.pallas_call(
        functools.partial(_sb_prompt_kernel, blk=blk, heads_per_step=heads_per_step),
        grid_spec=pltpu.PrefetchScalarGridSpec(
            num_scalar_prefetch=1,
            grid=grid,
            in_specs=[qo_spec(), kv_spec(), kv_spec()] + [slab_spec(w) for w in to_cast],
            out_specs=[qo_spec()] + [slab_spec(w) for w in to_cast],
        ),
        out_shape=[jax.ShapeDtypeStruct((B, L, sbw), F32)]
                  + [jax.ShapeDtypeStruct(w.shape, BF16) for w in to_cast],
        compiler_params=_cparams("arbitrary", "arbitrary", "arbitrary"),
        name="sb_prompt",
    )(sb_bias, q_b, k_b, v_b, *to_cast)
    return out[0], out[1:]


SUBLANES = 8


def _page_weights(qm, bias2, k_ref):
    page, n_heads, hd = k_ref.shape
    n = page * n_heads
    lane = lax.broadcasted_iota(jnp.int32, (n_heads, n), 1)
    sub = lax.broadcasted_iota(jnp.int32, (n_heads, n), 0)
    kp = k_ref[...].reshape(n, hd).astype(BF16)
    z = lax.dot_general(qm, kp, (((1,), (1,)), ((), ())), preferred_element_type=F32) + bias2
    ls, lk = _log2_sigmoid_pair(z)
    suffix = lk
    shift = n_heads
    while shift < n:
        ahead = pltpu.roll(suffix, n - shift, axis=1)
        suffix = suffix + jnp.where(lane < n - shift, ahead, 0.0)
        shift *= 2
    total = jnp.sum(jnp.where(lane == sub, suffix, 0.0), axis=1, keepdims=True)
    a = jnp.where(lane % n_heads == sub, jnp.exp2(ls + (suffix - lk)), 0.0)
    hi, lo = _split_bf16(a)
    return jnp.concatenate([hi, lo], axis=0), total


def _page_output(weights, v_ref):
    page, n_heads, hd = v_ref.shape
    o2 = jnp.dot(weights, v_ref[...].reshape(page * n_heads, hd).astype(BF16), preferred_element_type=F32)
    return o2[:n_heads] + o2[n_heads:]


COMBINE_SEQS = 8


def _decode_combine_kernel(pacc_ref, ptot_ref, o_ref, *, pages_per_seq):
    for s in range(o_ref.shape[0]):
        last = (s + 1) * pages_per_seq - 1

        def body(t, carry, last=last):
            run, acc = carry
            return run + ptot_ref[last - t], acc + jnp.exp2(run) * pacc_ref[last - t]

        zero = jnp.zeros(o_ref.shape[1:], F32)
        _, acc = lax.fori_loop(0, pages_per_seq, body, (zero, zero))
        o_ref[s] = acc


def _decode_combine(pacc, ptot, n_seq, pages_per_seq):
    _, n_heads, hd = pacc.shape
    group = math.gcd(n_seq, COMBINE_SEQS)
    seq_pages = lambda: pl.BlockSpec((group * pages_per_seq, n_heads, hd), lambda b: (b, 0, 0))
    out = pl.pallas_call(
        functools.partial(_decode_combine_kernel, pages_per_seq=pages_per_seq),
        grid=(n_seq // group,),
        in_specs=[seq_pages(), seq_pages()],
        out_specs=pl.BlockSpec((group, n_heads, hd), lambda b: (b, 0, 0)),
        out_shape=jax.ShapeDtypeStruct((n_seq, n_heads, hd), F32),
        compiler_params=_cparams("arbitrary"),
        name="sb_decode_combine",
    )(pacc, ptot)
    return out.reshape(n_seq, n_heads * hd)


def _cast_once(w_ref, w_scr):
    @pl.when(pl.program_id(0) == 0)
    def _():
        w_scr[...] = w_ref[...].astype(BF16)


def _out_proj_kernel(osb_ref, ocm_ref, h_ref, gsb_ref, w_ref, gffn_ref, h1_ref, f_ref, o_scr, w_scr):
    _cast_once(w_ref, w_scr)
    sbw = osb_ref.shape[1]
    o_scr[:, :sbw] = (_rms_rows(osb_ref[...]) * gsb_ref[...]).astype(BF16)
    o_scr[:, sbw:] = ocm_ref[...]
    h1 = h_ref[...] + jnp.dot(o_scr[...], w_scr[...], preferred_element_type=F32)
    h1_ref[...] = h1
    f_ref[...] = (_rms_rows(h1) * gffn_ref[...]).astype(BF16)


def _out_proj(o_sb, o_cm_n, h, g_out_sb, w_out, g_ffn, *, tm):
    T, d_model = h.shape
    sbw, cmw = o_sb.shape[1], o_cm_n.shape[1]
    assert T % tm == 0 and w_out.shape == (sbw + cmw, d_model)
    row_blk = lambda i: (i, 0)
    const2 = lambda i: (0, 0)
    return pl.pallas_call(
        _out_proj_kernel,
        grid=(T // tm,),
        in_specs=[pl.BlockSpec((tm, sbw), row_blk),
                  pl.BlockSpec((tm, cmw), row_blk),
                  pl.BlockSpec((tm, d_model), row_blk),
                  pl.BlockSpec((1, sbw), const2),
                  _resident((sbw + cmw, d_model)),
                  pl.BlockSpec((1, d_model), const2)],
        out_specs=[pl.BlockSpec((tm, d_model), row_blk), pl.BlockSpec((tm, d_model), row_blk)],
        out_shape=[jax.ShapeDtypeStruct((T, d_model), F32), jax.ShapeDtypeStruct((T, d_model), BF16)],
        scratch_shapes=[pltpu.VMEM((tm, sbw + cmw), BF16), pltpu.VMEM((sbw + cmw, d_model), BF16)],
        compiler_params=_cparams("arbitrary"),
        name="out_proj",
    )(o_sb, o_cm_n, h, g_out_sb, w_out, g_ffn)


def _ffn_kernel(*refs, side_pages, pages_per_seq, total_pages):
    if side_pages:
        slot_page_ref, f_ref, wg_ref, wu_ref, wd_ref, q_ref, bias_ref = refs[:7]
        k_refs = refs[7:7 + side_pages]
        v_refs = refs[7 + side_pages:7 + 2 * side_pages]
        delta_ref, pacc_ref, ptot_ref = refs[7 + 2 * side_pages:]
    else:
        f_ref, wg_ref, wu_ref, wd_ref, delta_ref = refs
    tf = wd_ref.shape[0]
    i, j = pl.program_id(0), pl.program_id(1)

    @pl.when(j == 0)
    def _():
        delta_ref[...] = jnp.zeros_like(delta_ref)

    gate_up = jnp.dot(f_ref[...], jnp.concatenate([wg_ref[...], wu_ref[...]], axis=1),
                      preferred_element_type=F32)
    gate, up = gate_up[:, :tf], gate_up[:, tf:]
    if side_pages:
        first = (i * pl.num_programs(1) + j) * side_pages
        bias2 = bias_ref[...] * LOG2E
        weights = []
        for r in range(side_pages):
            seq = jnp.minimum(first + r, total_pages - 1) // pages_per_seq
            w, total = _page_weights(q_ref[seq].astype(BF16), bias2, k_refs[r])
            ptot_ref[r] = jnp.broadcast_to(total, ptot_ref.shape[1:])
            weights.append(w)
    act = (gate * jax.nn.sigmoid(gate) * up).astype(BF16)
    delta_ref[...] += jnp.dot(act, wd_ref[...], preferred_element_type=F32)
    if side_pages:
        for r in range(side_pages):
            pacc_ref[r] = _page_output(weights[r], v_refs[r])


def _ffn(f, wg_b, wu_b, wd_b, *, tm, tf, side=None, side_pages=0):
    T, d_model = f.shape
    d_ff = wd_b.shape[0]
    assert T % tm == 0 and d_ff % tf == 0
    grid = (T // tm, d_ff // tf)
    in_specs = [pl.BlockSpec((tm, d_model), lambda i, j, *_: (i, 0),
                             pipeline_mode=pl.Buffered(1) if side is not None else None),
                pl.BlockSpec((d_model, tf), lambda i, j, *_: (0, j)),
                pl.BlockSpec((d_model, tf), lambda i, j, *_: (0, j)),
                pl.BlockSpec((tf, d_model), lambda i, j, *_: (j, 0))]
    out_specs = [pl.BlockSpec((tm, d_model), lambda i, j, *_: (i, 0))]
    out_shape = [jax.ShapeDtypeStruct((T, d_model), F32)]
    args = [f, wg_b, wu_b, wd_b]
    if side is None:
        return pl.pallas_call(
            functools.partial(_ffn_kernel, side_pages=0, pages_per_seq=0, total_pages=0),
            grid=grid, in_specs=in_specs, out_specs=out_specs, out_shape=out_shape,
            compiler_params=_cparams("arbitrary", "arbitrary"), name="ffn",
        )(*args)[0]

    q, sb_bias, cache_k, cache_v, layer, page_table = side
    n_seq, pages_per_seq = page_table.shape
    _, _, page, n_heads, hd = cache_k.shape
    n = page * n_heads
    total_pages = n_seq * pages_per_seq
    n_steps = grid[0] * grid[1]
    assert n_heads == SUBLANES and hd == HEAD_DIM and n & (n - 1) == 0
    assert n_steps * side_pages >= total_pages

    slot_page = jnp.pad(page_table.reshape(-1), (0, n_steps * side_pages - total_pages), mode="edge")

    def page_spec(r):
        return pl.BlockSpec((None, None, page, n_heads, hd),
                            lambda i, j, sp: (layer, sp[(i * grid[1] + j) * side_pages + r], 0, 0, 0))

    step_pages = lambda: pl.BlockSpec((side_pages, n_heads, hd), lambda i, j, pt: (i * grid[1] + j, 0, 0))
    part = jax.ShapeDtypeStruct((n_steps * side_pages, n_heads, hd), F32)
    delta, pacc, ptot = pl.pallas_call(
        functools.partial(_ffn_kernel, side_pages=side_pages, pages_per_seq=pages_per_seq,
                          total_pages=total_pages),
        grid_spec=pltpu.PrefetchScalarGridSpec(
            num_scalar_prefetch=1,
            grid=grid,
            in_specs=in_specs
                     + [pl.BlockSpec((n_seq, n_heads, hd), lambda i, j, pt: (0, 0, 0)),
                        pl.BlockSpec((n_heads, n), lambda i, j, pt: (0, 0))]
                     + [page_spec(r) for r in range(side_pages)] * 2,
            out_specs=out_specs + [step_pages(), step_pages()],
        ),
        out_shape=out_shape + [part, part],
        compiler_params=_cparams("arbitrary", "arbitrary"),
        name="ffn_with_decode_pages",
    )(slot_page, *args, q, jnp.broadcast_to(sb_bias[:, None], (n_heads, n)),
      *([cache_k] * side_pages), *([cache_v] * side_pages))
    return delta, _decode_combine(pacc, ptot, n_seq, pages_per_seq)


def _ple_kernel(h1_ref, delta_ref, p_ref, gple_ref, wg_ref, wp_ref, o_ref, wg_scr, wp_scr):
    _cast_once(wg_ref, wg_scr)
    _cast_once(wp_ref, wp_scr)
    h2 = h1_ref[...] + delta_ref[...]
    hn = (_rms_rows(h2) * gple_ref[...]).astype(BF16)
    gate = jax.nn.sigmoid(jnp.dot(hn, wg_scr[...], preferred_element_type=F32))
    proj = jnp.dot(p_ref[...].astype(BF16), wp_scr[...], preferred_element_type=F32)
    o_ref[...] = h2 + gate * proj


def _ple(h1, delta, p, g_ple, w_gate, w_proj, *, tm):
    T, d_model = h1.shape
    ple_dim = p.shape[1]
    assert T % tm == 0
    row_blk = lambda i: (i, 0)
    const2 = lambda i: (0, 0)
    return pl.pallas_call(
        _ple_kernel,
        grid=(T // tm,),
        in_specs=[pl.BlockSpec((tm, d_model), row_blk),
                  pl.BlockSpec((tm, d_model), row_blk),
                  pl.BlockSpec((tm, ple_dim), row_blk),
                  pl.BlockSpec((1, d_model), const2),
                  _resident((d_model, d_model)),
                  _resident((ple_dim, d_model))],
        out_specs=pl.BlockSpec((tm, d_model), row_blk),
        out_shape=jax.ShapeDtypeStruct((T, d_model), F32),
        scratch_shapes=[pltpu.VMEM((d_model, d_model), BF16), pltpu.VMEM((ple_dim, d_model), BF16)],
        compiler_params=_cparams("arbitrary"),
        name="ple",
    )(h1, delta, p, g_ple, w_gate, w_proj)


PROMPT_FFN_ROWS = 1024
FFN_COLS = 256


def kernel(x_prompt, x_sample, cache_k, cache_v, page_table, p_prompt, p_sample, g_mix, w_in, g_q, g_k, sb_bias, sgu_ln_g, sgu_ln_b, w_s, b_s, g_out_sb, g_out_cm, w_out, g_ffn, w_gate, w_up, w_down, g_ple, w_ple_gate, w_ple_proj):
    depth = w_in.shape[0]
    B, L, d_model = x_prompt.shape
    Bs, Ls, _ = x_sample.shape
    n_sb = cache_k.shape[3]
    n_cm = w_s.shape[1]
    d_ff = w_gate.shape[2]
    sbw, cmw = n_sb * HEAD_DIM, n_cm * HEAD_DIM
    assert Ls == 1 and n_sb == n_cm and cache_k.shape[4] == HEAD_DIM
    assert (page_table.shape[1] * cache_k.shape[2]) % CHUNK == 0 and L % CHUNK == 0
    ffn_steps = (B * L // PROMPT_FFN_ROWS) * (d_ff // FFN_COLS)
    side_pages = -(-page_table.size // ffn_steps)

    hp = x_prompt.reshape(B * L, d_model)
    hs = x_sample.reshape(Bs * Ls, d_model)
    row = lambda a: a.reshape(1, -1)
    kp_l, vp_l, ks_l, vs_l, cs_l = [], [], [], [], []
    for i in range(depth):
        w_in_b = w_in[i].astype(BF16)
        proj_args = (row(g_mix[i]), w_in_b, row(g_q[i]), row(g_k[i]), row(sgu_ln_g[i]), row(sgu_ln_b[i]),
                     w_s[i], b_s[i].T, row(g_out_cm[i]))
        out_args = (row(g_out_sb[i]), w_out[i], row(g_ffn[i]))
        ple_args = (row(g_ple[i]), w_ple_gate[i], w_ple_proj[i])

        q_b, k_f, k_b, v_f, v_b, ocm_n = _in_proj(hp, *proj_args, tm=512, fresh_chunk_rows=False)
        q_s, ks_f, _, vs_f, _, ocm_s, cs = _in_proj(hs, *proj_args, tm=Bs, fresh_chunk_rows=True)

        o_sb, (wg_b, wu_b, wd_b) = _sb_prompt(
            q_b.reshape(B, L, sbw), k_b.reshape(B, L, sbw), v_b.reshape(B, L, sbw), sb_bias[i],
            to_cast=(w_gate[i], w_up[i], w_down[i]), blk=256, heads_per_step=4)
        h1, f = _out_proj(o_sb.reshape(B * L, sbw), ocm_n, hp, *out_args, tm=256)
        delta, o_sb_s = _ffn(f, wg_b, wu_b, wd_b, tm=PROMPT_FFN_ROWS, tf=FFN_COLS, side_pages=side_pages,
                             side=(q_s.reshape(Bs, n_sb, HEAD_DIM), sb_bias[i], cache_k, cache_v, i, page_table))
        hp = _ple(h1, delta, p_prompt[i].reshape(B * L, -1), *ple_args, tm=256)
        kp_l.append(k_f.reshape(B, L, n_sb, HEAD_DIM))
        vp_l.append(v_f.reshape(B, L, n_sb, HEAD_DIM))

        h1_s, f_s = _out_proj(o_sb_s, ocm_s, hs, *out_args, tm=Bs)
        delta_s = _ffn(f_s, wg_b, wu_b, wd_b, tm=Bs, tf=2 * FFN_COLS)
        hs = _ple(h1_s, delta_s, p_sample[i].reshape(Bs, -1), *ple_args, tm=Bs)
        ks_l.append(ks_f.reshape(Bs, Ls, n_sb, HEAD_DIM))
        vs_l.append(vs_f.reshape(Bs, Ls, n_sb, HEAD_DIM))
        cs_l.append(cs.reshape(Bs, Ls, n_cm, HEAD_DIM))

    return (hp.reshape(B, L, d_model), hs.reshape(Bs, Ls, d_model),
            jnp.stack(kp_l), jnp.stack(vp_l), jnp.stack(ks_l), jnp.stack(vs_l), jnp.stack(cs_l))
```

```python
import functools
import math

import jax
import jax.numpy as jnp
from jax import lax
from jax.experimental import pallas as pl
from jax.experimental.pallas import tpu as pltpu

HEAD_DIM = 128
CHUNK = 128
EPS = 1e-6
SB_SCALE = 1.0 / math.sqrt(HEAD_DIM)
INV_SQRT2 = 0.7071067811865476
LOG2E = 1.4426950408889634
Q_SCALE = SB_SCALE * LOG2E

LANES = 128
VMEM_LIMIT = 56 * 1024 * 1024

F32 = jnp.float32
BF16 = jnp.bfloat16


def _cparams(*sem):
    return pltpu.CompilerParams(dimension_semantics=sem, vmem_limit_bytes=VMEM_LIMIT)


def _resident(shape):
    return pl.BlockSpec(shape, lambda i: (0,) * len(shape), pipeline_mode=pl.Buffered(1))


def _rms_rows(x):
    return x * lax.rsqrt(jnp.mean(x * x, axis=-1, keepdims=True) + EPS)


def _gelu(x):
    return 0.5 * x * (1.0 + lax.erf(x * INV_SQRT2))


def _log2_sigmoid_pair(z2):
    ls = jnp.minimum(z2, 0.0) - jnp.log2(1.0 + jnp.exp2(-jnp.abs(z2)))
    return ls, ls - z2


def _split_bf16(x):
    hi = x.astype(BF16)
    lo = (x - hi.astype(F32)).astype(BF16)
    return hi, lo


def _inproj_kernel(x_ref, gmix_ref, w_ref, gq_ref, gk_ref, lng_ref, lnb_ref, ws_ref, bs_ref, gcm_ref,
                   q_ref, kf_ref, kb_ref, vf_ref, vb_ref, ocm_ref, *rest, n_heads, fresh_chunk_rows):
    cs_ref = rest[0] if fresh_chunk_rows else None
    xn_ref, u_ref = rest[-2:]
    xn_ref[...] = (_rms_rows(x_ref[...]) * gmix_ref[...]).astype(BF16)

    tm = xn_ref.shape[0]
    heads_per_dot = 4

    def heads_of(group):
        slab = heads_per_dot * HEAD_DIM
        for p in range(n_heads // heads_per_dot):
            start = group * n_heads * HEAD_DIM + p * slab
            zp = jnp.dot(xn_ref[...], w_ref[:, start:start + slab], preferred_element_type=F32)
            for hh in range(heads_per_dot):
                yield heads_per_dot * p + hh, zp[:, hh * HEAD_DIM:(hh + 1) * HEAD_DIM]

    def cols(h):
        return slice(h * HEAD_DIM, (h + 1) * HEAD_DIM)

    for h, z in heads_of(0):
        q = _rms_rows(z) * gq_ref[...]
        q_ref[:, cols(h)] = (q * Q_SCALE).astype(q_ref.dtype)

    for h, z in heads_of(1):
        k = _rms_rows(z) * gk_ref[...]
        kf_ref[:, cols(h)] = k
        kb_ref[:, cols(h)] = k.astype(BF16)

    for h, z in heads_of(2):
        vf_ref[:, cols(h)] = z
        vb_ref[:, cols(h)] = z.astype(BF16)

    for h, z in heads_of(3):
        u_ref[:, cols(h)] = _gelu(z)

    if not fresh_chunk_rows:
        row = lax.broadcasted_iota(jnp.int32, (CHUNK, CHUNK), 0)
        col = lax.broadcasted_iota(jnp.int32, (CHUNK, CHUNK), 1)
        tril = col <= row
    for h, z in heads_of(4):
        gh = _gelu(z)
        mu = jnp.mean(gh, axis=-1, keepdims=True)
        xc = gh - mu
        vc = xc * lax.rsqrt(jnp.mean(xc * xc, axis=-1, keepdims=True) + EPS)
        vc = vc * lng_ref[:, cols(h)] + lnb_ref[:, cols(h)]
        if fresh_chunk_rows:
            cs_ref[:, cols(h)] = vc
            mixed = ws_ref[h, 0:1, 0:1] * vc + bs_ref[0:1, h:h + 1]
            u_ref[:, cols(h)] = u_ref[:, cols(h)] * mixed
        else:
            wm = jnp.where(tril, ws_ref[h], 0.0).astype(BF16)
            for c in range(tm // CHUNK):
                rows = slice(c * CHUNK, (c + 1) * CHUNK)
                mixed = jnp.dot(wm, vc[rows].astype(BF16), preferred_element_type=F32)
                mixed = mixed + bs_ref[:, h:h + 1]
                u_ref[rows, cols(h)] = u_ref[rows, cols(h)] * mixed
    ocm_ref[...] = (_rms_rows(u_ref[...]) * gcm_ref[...]).astype(BF16)


def _in_proj(x, g_mix, w_in_b, g_q, g_k, ln_g, ln_b, w_s, b_s_t, g_out_cm, *, tm, fresh_chunk_rows):
    T, d_model = x.shape
    n_heads = w_s.shape[0]
    gw = n_heads * HEAD_DIM
    assert w_in_b.shape == (d_model, 5 * gw) and T % tm == 0
    assert fresh_chunk_rows or tm % CHUNK == 0
    row_blk = lambda i: (i, 0)
    const2 = lambda i: (0, 0)
    out_f32 = jax.ShapeDtypeStruct((T, gw), F32)
    out_b16 = jax.ShapeDtypeStruct((T, gw), BF16)
    rows = lambda: pl.BlockSpec((tm, gw), row_blk)
    out_shape = [out_f32 if fresh_chunk_rows else out_b16, out_f32, out_b16, out_f32, out_b16, out_b16]
    if fresh_chunk_rows:
        out_shape.append(out_f32)
    return pl.pallas_call(
        functools.partial(_inproj_kernel, n_heads=n_heads, fresh_chunk_rows=fresh_chunk_rows),
        grid=(T // tm,),
        in_specs=[
            pl.BlockSpec((tm, d_model), row_blk),
            pl.BlockSpec((1, d_model), const2),
            _resident((d_model, 5 * gw)),
            pl.BlockSpec((1, HEAD_DIM), const2),
            pl.BlockSpec((1, HEAD_DIM), const2),
            pl.BlockSpec((1, gw), const2),
            pl.BlockSpec((1, gw), const2),
            pl.BlockSpec((n_heads, CHUNK, CHUNK), lambda i: (0, 0, 0)),
            pl.BlockSpec((CHUNK, n_heads), const2),
            pl.BlockSpec((1, gw), const2),
        ],
        out_specs=[rows() for _ in out_shape],
        out_shape=out_shape,
        scratch_shapes=[pltpu.VMEM((tm, d_model), BF16), pltpu.VMEM((tm, gw), F32)],
        compiler_params=_cparams("arbitrary"),
        name="in_proj",
    )(x, g_mix, w_in_b, g_q, g_k, ln_g, ln_b, w_s, b_s_t, g_out_cm)


def _sb_prompt_kernel(bias_ref, q_ref, k_ref, v_ref, *rest, blk, heads_per_step):
    n_cast = (len(rest) - 1) // 2
    o_ref = rest[n_cast]
    for w_ref, wb_ref in zip(rest[:n_cast], rest[n_cast + 1:]):
        wb_ref[...] = w_ref[...].astype(BF16)
    hg = pl.program_id(1)
    i = pl.program_id(2)
    row = lax.broadcasted_iota(jnp.int32, (blk, blk), 0)
    col = lax.broadcasted_iota(jnp.int32, (blk, blk), 1)
    later = jnp.where(row > col, 1.0, 0.0).astype(BF16)
    causal = col < row
    heads = range(heads_per_step)
    cols = [slice(g * HEAD_DIM, (g + 1) * HEAD_DIM) for g in heads]

    def block(j, runs, on_diagonal):
        start = pl.multiple_of(j * blk, blk)
        zs = [lax.dot_general(q_ref[0, :, cols[g]], k_ref[0, pl.ds(start, blk), cols[g]],
                              (((1,), (1,)), ((), ())), preferred_element_type=F32)
              + bias_ref[hg * heads_per_step + g] * LOG2E for g in heads]
        lss, lks = [], []
        for g in heads:
            ls, lk = _log2_sigmoid_pair(zs[g])
            lss.append(ls)
            lks.append(jnp.where(causal, lk, 0.0) if on_diagonal else lk)
        afters = [jnp.dot(lk.astype(BF16), later, preferred_element_type=F32) for lk in lks]
        weights = []
        for g in heads:
            a = jnp.exp2(lss[g] + afters[g] + runs[g])
            weights.append((jnp.where(causal, a, 0.0) if on_diagonal else a).astype(BF16))
        for g in heads:
            pv = jnp.dot(weights[g], v_ref[0, pl.ds(start, blk), cols[g]], preferred_element_type=F32)
            if on_diagonal:
                o_ref[0, :, cols[g]] = pv
            else:
                o_ref[0, :, cols[g]] += pv
        return tuple(runs[g] + jnp.sum(lks[g], axis=-1, keepdims=True) for g in heads)

    runs = block(i, (jnp.zeros((blk, 1), F32),) * heads_per_step, True)
    lax.fori_loop(0, i, lambda t, runs: block(i - 1 - t, runs, False), runs)


BF16_SUBLANES = 16


def _sb_prompt(q_b, k_b, v_b, sb_bias, to_cast=(), *, blk, heads_per_step):
    B, L, sbw = q_b.shape
    n_heads = sbw // HEAD_DIM
    assert L % blk == 0 and n_heads % heads_per_step == 0
    gw = heads_per_step * HEAD_DIM
    grid = (B, n_heads // heads_per_step, L // blk)
    n_steps = grid[0] * grid[1] * grid[2]
    qo_spec = lambda: pl.BlockSpec((1, blk, gw), lambda b, h, i, bias: (b, i, h))
    kv_spec = lambda: pl.BlockSpec((1, L, gw), lambda b, h, i, bias: (b, 0, h))

    def slab_spec(w):
        rows = next(r for r in range(BF16_SUBLANES, w.shape[0] + 1, BF16_SUBLANES)
                    if w.shape[0] % r == 0 and r * n_steps >= w.shape[0])
        last = w.shape[0] // rows - 1
        return pl.BlockSpec(
            (rows, w.shape[1]),
            lambda b, h, i, bias: (jnp.minimum((b * grid[1] + h) * grid[2] + i, last), 0))

    out = pl.pallas_call(
        functools.partial(_sb_prompt_kernel, blk=blk, heads_per_step=heads_per_step),
        grid_spec=pltpu.PrefetchScalarGridSpec(
            num_scalar_prefetch=1,
            grid=grid,
            in_specs=[qo_spec(), kv_spec(), kv_spec()] + [slab_spec(w) for w in to_cast],
            out_specs=[qo_spec()] + [slab_spec(w) for w in to_cast],
        ),
        out_shape=[jax.ShapeDtypeStruct((B, L, sbw), F32)]
                  + [jax.ShapeDtypeStruct(w.shape, BF16) for w in to_cast],
        compiler_params=_cparams("arbitrary", "arbitrary", "arbitrary"),
        name="sb_prompt",
    )(sb_bias, q_b, k_b, v_b, *to_cast)
    return out[0], out[1:]


SUBLANES = 8


def _page_weights(qm, bias2, k_ref):
    page, n_heads, hd = k_ref.shape
    n = page * n_heads
    lane = lax.broadcasted_iota(jnp.int32, (n_heads, n), 1)
    sub = lax.broadcasted_iota(jnp.int32, (n_heads, n), 0)
    kp = k_ref[...].reshape(n, hd).astype(BF16)
    z = lax.dot_general(qm, kp, (((1,), (1,)), ((), ())), preferred_element_type=F32) + bias2
    ls, lk = _log2_sigmoid_pair(z)
    suffix = lk
    shift = n_heads
    while shift < n:
        ahead = pltpu.roll(suffix, n - shift, axis=1)
        suffix = suffix + jnp.where(lane < n - shift, ahead, 0.0)
        shift *= 2
    total = jnp.sum(jnp.where(lane == sub, suffix, 0.0), axis=1, keepdims=True)
    a = jnp.where(lane % n_heads == sub, jnp.exp2(ls + (suffix - lk)), 0.0)
    hi, lo = _split_bf16(a)
    return jnp.concatenate([hi, lo], axis=0), total


def _page_output(weights, v_ref):
    page, n_heads, hd = v_ref.shape
    o2 = jnp.dot(weights, v_ref[...].reshape(page * n_heads, hd).astype(BF16), preferred_element_type=F32)
    return o2[:n_heads] + o2[n_heads:]


COMBINE_SEQS = 8


def _decode_combine_kernel(pacc_ref, ptot_ref, o_ref, *, pages_per_seq):
    for s in range(o_ref.shape[0]):
        last = (s + 1) * pages_per_seq - 1

        def body(t, carry, last=last):
            run, acc = carry
            return run + ptot_ref[last - t], acc + jnp.exp2(run) * pacc_ref[last - t]

        zero = jnp.zeros(o_ref.shape[1:], F32)
        _, acc = lax.fori_loop(0, pages_per_seq, body, (zero, zero))
        o_ref[s] = acc


def _decode_combine(pacc, ptot, n_seq, pages_per_seq):
    _, n_heads, hd = pacc.shape
    group = math.gcd(n_seq, COMBINE_SEQS)
    seq_pages = lambda: pl.BlockSpec((group * pages_per_seq, n_heads, hd), lambda b: (b, 0, 0))
    out = pl.pallas_call(
        functools.partial(_decode_combine_kernel, pages_per_seq=pages_per_seq),
        grid=(n_seq // group,),
        in_specs=[seq_pages(), seq_pages()],
        out_specs=pl.BlockSpec((group, n_heads, hd), lambda b: (b, 0, 0)),
        out_shape=jax.ShapeDtypeStruct((n_seq, n_heads, hd), F32),
        compiler_params=_cparams("arbitrary"),
        name="sb_decode_combine",
    )(pacc, ptot)
    return out.reshape(n_seq, n_heads * hd)


def _cast_once(w_ref, w_scr):
    @pl.when(pl.program_id(0) == 0)
    def _():
        w_scr[...] = w_ref[...].astype(BF16)


def _out_proj_kernel(osb_ref, ocm_ref, h_ref, gsb_ref, w_ref, gffn_ref, h1_ref, f_ref, o_scr, w_scr):
    _cast_once(w_ref, w_scr)
    sbw = osb_ref.shape[1]
    o_scr[:, :sbw] = (_rms_rows(osb_ref[...]) * gsb_ref[...]).astype(BF16)
    o_scr[:, sbw:] = ocm_ref[...]
    h1 = h_ref[...] + jnp.dot(o_scr[...], w_scr[...], preferred_element_type=F32)
    h1_ref[...] = h1
    f_ref[...] = (_rms_rows(h1) * gffn_ref[...]).astype(BF16)


def _out_proj(o_sb, o_cm_n, h, g_out_sb, w_out, g_ffn, *, tm):
    T, d_model = h.shape
    sbw, cmw = o_sb.shape[1], o_cm_n.shape[1]
    assert T % tm == 0 and w_out.shape == (sbw + cmw, d_model)
    row_blk = lambda i: (i, 0)
    const2 = lambda i: (0, 0)
    return pl.pallas_call(
        _out_proj_kernel,
        grid=(T // tm,),
        in_specs=[pl.BlockSpec((tm, sbw), row_blk),
                  pl.BlockSpec((tm, cmw), row_blk),
                  pl.BlockSpec((tm, d_model), row_blk),
                  pl.BlockSpec((1, sbw), const2),
                  _resident((sbw + cmw, d_model)),
                  pl.BlockSpec((1, d_model), const2)],
        out_specs=[pl.BlockSpec((tm, d_model), row_blk), pl.BlockSpec((tm, d_model), row_blk)],
        out_shape=[jax.ShapeDtypeStruct((T, d_model), F32), jax.ShapeDtypeStruct((T, d_model), BF16)],
        scratch_shapes=[pltpu.VMEM((tm, sbw + cmw), BF16), pltpu.VMEM((sbw + cmw, d_model), BF16)],
        compiler_params=_cparams("arbitrary"),
        name="out_proj",
    )(o_sb, o_cm_n, h, g_out_sb, w_out, g_ffn)


def _ffn_kernel(*refs, side_pages, pages_per_seq, total_pages):
    if side_pages:
        slot_page_ref, f_ref, wg_ref, wu_ref, wd_ref, q_ref, bias_ref = refs[:7]
        k_refs = refs[7:7 + side_pages]
        v_refs = refs[7 + side_pages:7 + 2 * side_pages]
        delta_ref, pacc_ref, ptot_ref = refs[7 + 2 * side_pages:]
    else:
        f_ref, wg_ref, wu_ref, wd_ref, delta_ref = refs
    tf = wd_ref.shape[0]
    i, j = pl.program_id(0), pl.program_id(1)

    @pl.when(j == 0)
    def _():
        delta_ref[...] = jnp.zeros_like(delta_ref)

    gate_up = jnp.dot(f_ref[...], jnp.concatenate([wg_ref[...], wu_ref[...]], axis=1),
                      preferred_element_type=F32)
    gate, up = gate_up[:, :tf], gate_up[:, tf:]
    if side_pages:
        first = (i * pl.num_programs(1) + j) * side_pages
        bias2 = bias_ref[...] * LOG2E
        weights = []
        for r in range(side_pages):
            seq = jnp.minimum(first + r, total_pages - 1) // pages_per_seq
            w, total = _page_weights(q_ref[seq].astype(BF16), bias2, k_refs[r])
            ptot_ref[r] = jnp.broadcast_to(total, ptot_ref.shape[1:])
            weights.append(w)
    act = (gate * jax.nn.sigmoid(gate) * up).astype(BF16)
    delta_ref[...] += jnp.dot(act, wd_ref[...], preferred_element_type=F32)
    if side_pages:
        for r in range(side_pages):
            pacc_ref[r] = _page_output(weights[r], v_refs[r])


def _ffn(f, wg_b, wu_b, wd_b, *, tm, tf, side=None, side_pages=0):
    T, d_model = f.shape
    d_ff = wd_b.shape[0]
    assert T % tm == 0 and d_ff % tf == 0
    grid = (T // tm, d_ff // tf)
    in_specs = [pl.BlockSpec((tm, d_model), lambda i, j, *_: (i, 0),
                             pipeline_mode=pl.Buffered(1) if side is not None else None),
                pl.BlockSpec((d_model, tf), lambda i, j, *_: (0, j)),
                pl.BlockSpec((d_model, tf), lambda i, j, *_: (0, j)),
                pl.BlockSpec((tf, d_model), lambda i, j, *_: (j, 0))]
    out_specs = [pl.BlockSpec((tm, d_model), lambda i, j, *_: (i, 0))]
    out_shape = [jax.ShapeDtypeStruct((T, d_model), F32)]
    args = [f, wg_b, wu_b, wd_b]
    if side is None:
        return pl.pallas_call(
            functools.partial(_ffn_kernel, side_pages=0, pages_per_seq=0, total_pages=0),
            grid=grid, in_specs=in_specs, out_specs=out_specs, out_shape=out_shape,
            compiler_params=_cparams("arbitrary", "arbitrary"), name="ffn",
        )(*args)[0]

    q, sb_bias, cache_k, cache_v, layer, page_table = side
    n_seq, pages_per_seq = page_table.shape
    _, _, page, n_heads, hd = cache_k.shape
    n = page * n_heads
    total_pages = n_seq * pages_per_seq
    n_steps = grid[0] * grid[1]
    assert n_heads == SUBLANES and hd == HEAD_DIM and n & (n - 1) == 0
    assert n_steps * side_pages >= total_pages

    slot_page = jnp.pad(page_table.reshape(-1), (0, n_steps * side_pages - total_pages), mode="edge")

    def page_spec(r):
        return pl.BlockSpec((None, None, page, n_heads, hd),
                            lambda i, j, sp: (layer, sp[(i * grid[1] + j) * side_pages + r], 0, 0, 0))

    step_pages = lambda: pl.BlockSpec((side_pages, n_heads, hd), lambda i, j, pt: (i * grid[1] + j, 0, 0))
    part = jax.ShapeDtypeStruct((n_steps * side_pages, n_heads, hd), F32)
    delta, pacc, ptot = pl.pallas_call(
        functools.partial(_ffn_kernel, side_pages=side_pages, pages_per_seq=pages_per_seq,
                          total_pages=total_pages),
        grid_spec=pltpu.PrefetchScalarGridSpec(
            num_scalar_prefetch=1,
            grid=grid,
            in_specs=in_specs
                     + [pl.BlockSpec((n_seq, n_heads, hd), lambda i, j, pt: (0, 0, 0)),
                        pl.BlockSpec((n_heads, n), lambda i, j, pt: (0, 0))]
                     + [page_spec(r) for r in range(side_pages)] * 2,
            out_specs=out_specs + [step_pages(), step_pages()],
        ),
        out_shape=out_shape + [part, part],
        compiler_params=_cparams("arbitrary", "arbitrary"),
        name="ffn_with_decode_pages",
    )(slot_page, *args, q, jnp.broadcast_to(sb_bias[:, None], (n_heads, n)),
      *([cache_k] * side_pages), *([cache_v] * side_pages))
    return delta, _decode_combine(pacc, ptot, n_seq, pages_per_seq)


def _ple_kernel(h1_ref, delta_ref, p_ref, gple_ref, wg_ref, wp_ref, o_ref, wg_scr, wp_scr):
    _cast_once(wg_ref, wg_scr)
    _cast_once(wp_ref, wp_scr)
    h2 = h1_ref[...] + delta_ref[...]
    hn = (_rms_rows(h2) * gple_ref[...]).astype(BF16)
    gate = jax.nn.sigmoid(jnp.dot(hn, wg_scr[...], preferred_element_type=F32))
    proj = jnp.dot(p_ref[...].astype(BF16), wp_scr[...], preferred_element_type=F32)
    o_ref[...] = h2 + gate * proj


def _ple(h1, delta, p, g_ple, w_gate, w_proj, *, tm):
    T, d_model = h1.shape
    ple_dim = p.shape[1]
    assert T % tm == 0
    row_blk = lambda i: (i, 0)
    const2 = lambda i: (0, 0)
    return pl.pallas_call(
        _ple_kernel,
        grid=(T // tm,),
        in_specs=[pl.BlockSpec((tm, d_model), row_blk),
                  pl.BlockSpec((tm, d_model), row_blk),
                  pl.BlockSpec((tm, ple_dim), row_blk),
                  pl.BlockSpec((1, d_model), const2),
                  _resident((d_model, d_model)),
                  _resident((ple_dim, d_model))],
        out_specs=pl.BlockSpec((tm, d_model), row_blk),
        out_shape=jax.ShapeDtypeStruct((T, d_model), F32),
        scratch_shapes=[pltpu.VMEM((d_model, d_model), BF16), pltpu.VMEM((ple_dim, d_model), BF16)],
        compiler_params=_cparams("arbitrary"),
        name="ple",
    )(h1, delta, p, g_ple, w_gate, w_proj)


PROMPT_FFN_ROWS = 1024
FFN_COLS = 256


def kernel(x_prompt, x_sample, cache_k, cache_v, page_table, p_prompt, p_sample, g_mix, w_in, g_q, g_k, sb_bias, sgu_ln_g, sgu_ln_b, w_s, b_s, g_out_sb, g_out_cm, w_out, g_ffn, w_gate, w_up, w_down, g_ple, w_ple_gate, w_ple_proj):
    depth = w_in.shape[0]
    B, L, d_model = x_prompt.shape
    Bs, Ls, _ = x_sample.shape
    n_sb = cache_k.shape[3]
    n_cm = w_s.shape[1]
    d_ff = w_gate.shape[2]
    sbw, cmw = n_sb * HEAD_DIM, n_cm * HEAD_DIM
    assert Ls == 1 and n_sb == n_cm and cache_k.shape[4] == HEAD_DIM
    assert (page_table.shape[1] * cache_k.shape[2]) % CHUNK == 0 and L % CHUNK == 0
    ffn_steps = (B * L // PROMPT_FFN_ROWS) * (d_ff // FFN_COLS)
    side_pages = -(-page_table.size // ffn_steps)

    hp = x_prompt.reshape(B * L, d_model)
    hs = x_sample.reshape(Bs * Ls, d_model)
    row = lambda a: a.reshape(1, -1)
    kp_l, vp_l, ks_l, vs_l, cs_l = [], [], [], [], []
    for i in range(depth):
        w_in_b = w_in[i].astype(BF16)
        proj_args = (row(g_mix[i]), w_in_b, row(g_q[i]), row(g_k[i]), row(sgu_ln_g[i]), row(sgu_ln_b[i]),
                     w_s[i], b_s[i].T, row(g_out_cm[i]))
        out_args = (row(g_out_sb[i]), w_out[i], row(g_ffn[i]))
        ple_args = (row(g_ple[i]), w_ple_gate[i], w_ple_proj[i])

        q_b, k_f, k_b, v_f, v_b, ocm_n = _in_proj(hp, *proj_args, tm=512, fresh_chunk_rows=False)
        q_s, ks_f, _, vs_f, _, ocm_s, cs = _in_proj(hs, *proj_args, tm=Bs, fresh_chunk_rows=True)

        o_sb, (wg_b, wu_b, wd_b) = _sb_prompt(
            q_b.reshape(B, L, sbw), k_b.reshape(B, L, sbw), v_b.reshape(B, L, sbw), sb_bias[i],
            to_cast=(w_gate[i], w_up[i], w_down[i]), blk=256, heads_per_step=8)
        h1, f = _out_proj(o_sb.reshape(B * L, sbw), ocm_n, hp, *out_args, tm=512)
        delta, o_sb_s = _ffn(f, wg_b, wu_b, wd_b, tm=PROMPT_FFN_ROWS, tf=FFN_COLS, side_pages=side_pages,
                             side=(q_s.reshape(Bs, n_sb, HEAD_DIM), sb_bias[i], cache_k, cache_v, i, page_table))
        hp = _ple(h1, delta, p_prompt[i].reshape(B * L, -1), *ple_args, tm=256)
        kp_l.append(k_f.reshape(B, L, n_sb, HEAD_DIM))
        vp_l.append(v_f.reshape(B, L, n_sb, HEAD_DIM))

        h1_s, f_s = _out_proj(o_sb_s, ocm_s, hs, *out_args, tm=Bs)
        delta_s = _ffn(f_s, wg_b, wu_b, wd_b, tm=Bs, tf=2 * FFN_COLS)
        hs = _ple(h1_s, delta_s, p_sample[i].reshape(Bs, -1), *ple_args, tm=Bs)
        ks_l.append(ks_f.reshape(Bs, Ls, n_sb, HEAD_DIM))
        vs_l.append(vs_f.reshape(Bs, Ls, n_sb, HEAD_DIM))
        cs_l.append(cs.reshape(Bs, Ls, n_cm, HEAD_DIM))

    return (hp.reshape(B, L, d_model), hs.reshape(Bs, Ls, d_model),
            jnp.stack(kp_l), jnp.stack(vp_l), jnp.stack(ks_l), jnp.stack(vs_l), jnp.stack(cs_l))
```

```python
import functools
import math

import jax
import jax.numpy as jnp
from jax import lax
from jax.experimental import pallas as pl
from jax.experimental.pallas import tpu as pltpu

HEAD_DIM = 128
CHUNK = 128
EPS = 1e-6
SB_SCALE = 1.0 / math.sqrt(HEAD_DIM)
INV_SQRT2 = 0.7071067811865476
LOG2E = 1.4426950408889634
Q_SCALE = SB_SCALE * LOG2E

LANES = 128
VMEM_LIMIT = 60 * 1024 * 1024

F32 = jnp.float32
BF16 = jnp.bfloat16


def _cparams(*sem):
    return pltpu.CompilerParams(dimension_semantics=sem, vmem_limit_bytes=VMEM_LIMIT)


def _resident(shape):
    return pl.BlockSpec(shape, lambda i: (0,) * len(shape), pipeline_mode=pl.Buffered(1))


def _rms_rows(x):
    return x * lax.rsqrt(jnp.mean(x * x, axis=-1, keepdims=True) + EPS)


def _gelu(x):
    return 0.5 * x * (1.0 + lax.erf(x * INV_SQRT2))


def _log2_sigmoid_pair(z2):
    ls = jnp.minimum(z2, 0.0) - jnp.log2(1.0 + jnp.exp2(-jnp.abs(z2)))
    return ls, ls - z2


def _split_bf16(x):
    hi = x.astype(BF16)
    lo = (x - hi.astype(F32)).astype(BF16)
    return hi, lo


def _inproj_kernel(x_ref, gmix_ref, w_ref, gq_ref, gk_ref, lng_ref, lnb_ref, ws_ref, bs_ref, gcm_ref,
                   q_ref, kf_ref, kb_ref, vf_ref, vb_ref, ocm_ref, *rest, n_heads, fresh_chunk_rows):
    cs_ref = rest[0] if fresh_chunk_rows else None
    xn_ref, u_ref = rest[-2:]
    xn_ref[...] = (_rms_rows(x_ref[...]) * gmix_ref[...]).astype(BF16)

    tm = xn_ref.shape[0]
    heads_per_dot = 4

    def heads_of(group):
        slab = heads_per_dot * HEAD_DIM
        for p in range(n_heads // heads_per_dot):
            start = group * n_heads * HEAD_DIM + p * slab
            zp = jnp.dot(xn_ref[...], w_ref[:, start:start + slab], preferred_element_type=F32)
            for hh in range(heads_per_dot):
                yield heads_per_dot * p + hh, zp[:, hh * HEAD_DIM:(hh + 1) * HEAD_DIM]

    def cols(h):
        return slice(h * HEAD_DIM, (h + 1) * HEAD_DIM)

    for h, z in heads_of(0):
        q = _rms_rows(z) * gq_ref[...]
        q_ref[:, cols(h)] = (q * Q_SCALE).astype(q_ref.dtype)

    for h, z in heads_of(1):
        k = _rms_rows(z) * gk_ref[...]
        kf_ref[:, cols(h)] = k
        kb_ref[:, cols(h)] = k.astype(BF16)

    for h, z in heads_of(2):
        vf_ref[:, cols(h)] = z
        vb_ref[:, cols(h)] = z.astype(BF16)

    for h, z in heads_of(3):
        u_ref[:, cols(h)] = _gelu(z)

    if not fresh_chunk_rows:
        row = lax.broadcasted_iota(jnp.int32, (CHUNK, CHUNK), 0)
        col = lax.broadcasted_iota(jnp.int32, (CHUNK, CHUNK), 1)
        tril = col <= row
    for h, z in heads_of(4):
        gh = _gelu(z)
        mu = jnp.mean(gh, axis=-1, keepdims=True)
        xc = gh - mu
        vc = xc * lax.rsqrt(jnp.mean(xc * xc, axis=-1, keepdims=True) + EPS)
        vc = vc * lng_ref[:, cols(h)] + lnb_ref[:, cols(h)]
        if fresh_chunk_rows:
            cs_ref[:, cols(h)] = vc
            mixed = ws_ref[h, 0:1, 0:1] * vc + bs_ref[0:1, h:h + 1]
            u_ref[:, cols(h)] = u_ref[:, cols(h)] * mixed
        else:
            wm = jnp.where(tril, ws_ref[h], 0.0).astype(BF16)
            for c in range(tm // CHUNK):
                rows = slice(c * CHUNK, (c + 1) * CHUNK)
                mixed = jnp.dot(wm, vc[rows].astype(BF16), preferred_element_type=F32)
                mixed = mixed + bs_ref[:, h:h + 1]
                u_ref[rows, cols(h)] = u_ref[rows, cols(h)] * mixed
    ocm_ref[...] = (_rms_rows(u_ref[...]) * gcm_ref[...]).astype(BF16)


def _in_proj(x, g_mix, w_in_b, g_q, g_k, ln_g, ln_b, w_s, b_s_t, g_out_cm, *, tm, fresh_chunk_rows):
    T, d_model = x.shape
    n_heads = w_s.shape[0]
    gw = n_heads * HEAD_DIM
    assert w_in_b.shape == (d_model, 5 * gw) and T % tm == 0
    assert fresh_chunk_rows or tm % CHUNK == 0
    row_blk = lambda i: (i, 0)
    const2 = lambda i: (0, 0)
    out_f32 = jax.ShapeDtypeStruct((T, gw), F32)
    out_b16 = jax.ShapeDtypeStruct((T, gw), BF16)
    rows = lambda: pl.BlockSpec((tm, gw), row_blk)
    out_shape = [out_f32 if fresh_chunk_rows else out_b16, out_f32, out_b16, out_f32, out_b16, out_b16]
    if fresh_chunk_rows:
        out_shape.append(out_f32)
    return pl.pallas_call(
        functools.partial(_inproj_kernel, n_heads=n_heads, fresh_chunk_rows=fresh_chunk_rows),
        grid=(T // tm,),
        in_specs=[
            pl.BlockSpec((tm, d_model), row_blk),
            pl.BlockSpec((1, d_model), const2),
            _resident((d_model, 5 * gw)),
            pl.BlockSpec((1, HEAD_DIM), const2),
            pl.BlockSpec((1, HEAD_DIM), const2),
            pl.BlockSpec((1, gw), const2),
            pl.BlockSpec((1, gw), const2),
            pl.BlockSpec((n_heads, CHUNK, CHUNK), lambda i: (0, 0, 0)),
            pl.BlockSpec((CHUNK, n_heads), const2),
            pl.BlockSpec((1, gw), const2),
        ],
        out_specs=[rows() for _ in out_shape],
        out_shape=out_shape,
        scratch_shapes=[pltpu.VMEM((tm, d_model), BF16), pltpu.VMEM((tm, gw), F32)],
        compiler_params=_cparams("arbitrary"),
        name="in_proj",
    )(x, g_mix, w_in_b, g_q, g_k, ln_g, ln_b, w_s, b_s_t, g_out_cm)


def _sb_prompt_kernel(bias_ref, q_ref, k_ref, v_ref, *rest, blk, heads_per_step):
    n_cast = (len(rest) - 1) // 2
    o_ref = rest[n_cast]
    for w_ref, wb_ref in zip(rest[:n_cast], rest[n_cast + 1:]):
        wb_ref[...] = w_ref[...].astype(BF16)
    hg = pl.program_id(1)
    i = pl.program_id(2)
    row = lax.broadcasted_iota(jnp.int32, (blk, blk), 0)
    col = lax.broadcasted_iota(jnp.int32, (blk, blk), 1)
    later = jnp.where(row > col, 1.0, 0.0).astype(BF16)
    causal = col < row
    heads = range(heads_per_step)
    cols = [slice(g * HEAD_DIM, (g + 1) * HEAD_DIM) for g in heads]

    def block(j, runs, on_diagonal):
        start = pl.multiple_of(j * blk, blk)
        zs = [lax.dot_general(q_ref[0, :, cols[g]], k_ref[0, pl.ds(start, blk), cols[g]],
                              (((1,), (1,)), ((), ())), preferred_element_type=F32)
              + bias_ref[hg * heads_per_step + g] * LOG2E for g in heads]
        lss, lks = [], []
        for g in heads:
            ls, lk = _log2_sigmoid_pair(zs[g])
            lss.append(ls)
            lks.append(jnp.where(causal, lk, 0.0) if on_diagonal else lk)
        afters = [jnp.dot(lk.astype(BF16), later, preferred_element_type=F32) for lk in lks]
        weights = []
        for g in heads:
            a = jnp.exp2(lss[g] + afters[g] + runs[g])
            weights.append((jnp.where(causal, a, 0.0) if on_diagonal else a).astype(BF16))
        for g in heads:
            pv = jnp.dot(weights[g], v_ref[0, pl.ds(start, blk), cols[g]], preferred_element_type=F32)
            if on_diagonal:
                o_ref[0, :, cols[g]] = pv
            else:
                o_ref[0, :, cols[g]] += pv
        return tuple(runs[g] + jnp.sum(lks[g], axis=-1, keepdims=True) for g in heads)

    runs = block(i, (jnp.zeros((blk, 1), F32),) * heads_per_step, True)
    lax.fori_loop(0, i, lambda t, runs: block(i - 1 - t, runs, False), runs)


BF16_SUBLANES = 16


def _sb_prompt(q_b, k_b, v_b, sb_bias, to_cast=(), *, blk, heads_per_step):
    B, L, sbw = q_b.shape
    n_heads = sbw // HEAD_DIM
    assert L % blk == 0 and n_heads % heads_per_step == 0
    gw = heads_per_step * HEAD_DIM
    grid = (B, n_heads // heads_per_step, L // blk)
    n_steps = grid[0] * grid[1] * grid[2]
    qo_spec = lambda: pl.BlockSpec((1, blk, gw), lambda b, h, i, bias: (b, i, h))
    kv_spec = lambda: pl.BlockSpec((1, L, gw), lambda b, h, i, bias: (b, 0, h))

    def slab_spec(w):
        rows = next(r for r in range(BF16_SUBLANES, w.shape[0] + 1, BF16_SUBLANES)
                    if w.shape[0] % r == 0 and r * n_steps >= w.shape[0])
        last = w.shape[0] // rows - 1
        return pl.BlockSpec(
            (rows, w.shape[1]),
            lambda b, h, i, bias: (jnp.minimum((b * grid[1] + h) * grid[2] + i, last), 0))

    out = pl.pallas_call(
        functools.partial(_sb_prompt_kernel, blk=blk, heads_per_step=heads_per_step),
        grid_spec=pltpu.PrefetchScalarGridSpec(
            num_scalar_prefetch=1,
            grid=grid,
            in_specs=[qo_spec(), kv_spec(), kv_spec()] + [slab_spec(w) for w in to_cast],
            out_specs=[qo_spec()] + [slab_spec(w) for w in to_cast],
        ),
        out_shape=[jax.ShapeDtypeStruct((B, L, sbw), F32)]
                  + [jax.ShapeDtypeStruct(w.shape, BF16) for w in to_cast],
        compiler_params=_cparams("arbitrary", "arbitrary", "arbitrary"),
        name="sb_prompt",
    )(sb_bias, q_b, k_b, v_b, *to_cast)
    return out[0], out[1:]


SUBLANES = 8


def _page_weights(qm, bias2, k_ref):
    page, n_heads, hd = k_ref.shape
    n = page * n_heads
    lane = lax.broadcasted_iota(jnp.int32, (n_heads, n), 1)
    sub = lax.broadcasted_iota(jnp.int32, (n_heads, n), 0)
    kp = k_ref[...].reshape(n, hd).astype(BF16)
    z = lax.dot_general(qm, kp, (((1,), (1,)), ((), ())), preferred_element_type=F32) + bias2
    ls, lk = _log2_sigmoid_pair(z)
    suffix = lk
    shift = n_heads
    while shift < n:
        ahead = pltpu.roll(suffix, n - shift, axis=1)
        suffix = suffix + jnp.where(lane < n - shift, ahead, 0.0)
        shift *= 2
    total = jnp.sum(jnp.where(lane == sub, suffix, 0.0), axis=1, keepdims=True)
    a = jnp.where(lane % n_heads == sub, jnp.exp2(ls + (suffix - lk)), 0.0)
    hi, lo = _split_bf16(a)
    return jnp.concatenate([hi, lo], axis=0), total


def _page_output(weights, v_ref):
    page, n_heads, hd = v_ref.shape
    o2 = jnp.dot(weights, v_ref[...].reshape(page * n_heads, hd).astype(BF16), preferred_element_type=F32)
    return o2[:n_heads] + o2[n_heads:]


COMBINE_SEQS = 8


def _decode_combine_kernel(pacc_ref, ptot_ref, o_ref, *, pages_per_seq):
    n_seq = o_ref.shape[0]

    def body(t, carry):
        out = []
        for s, (run, acc) in enumerate(carry):
            p = (s + 1) * pages_per_seq - 1 - t
            out.append((run + ptot_ref[p], acc + jnp.exp2(run) * pacc_ref[p]))
        return tuple(out)

    zero = jnp.zeros(o_ref.shape[1:], F32)
    final = lax.fori_loop(0, pages_per_seq, body, ((zero, zero),) * n_seq)
    for s in range(n_seq):
        o_ref[s] = final[s][1]


def _decode_combine(pacc, ptot, n_seq, pages_per_seq):
    _, n_heads, hd = pacc.shape
    group = math.gcd(n_seq, COMBINE_SEQS)
    seq_pages = lambda: pl.BlockSpec((group * pages_per_seq, n_heads, hd), lambda b: (b, 0, 0))
    out = pl.pallas_call(
        functools.partial(_decode_combine_kernel, pages_per_seq=pages_per_seq),
        grid=(n_seq // group,),
        in_specs=[seq_pages(), seq_pages()],
        out_specs=pl.BlockSpec((group, n_heads, hd), lambda b: (b, 0, 0)),
        out_shape=jax.ShapeDtypeStruct((n_seq, n_heads, hd), F32),
        compiler_params=_cparams("arbitrary"),
        name="sb_decode_combine",
    )(pacc, ptot)
    return out.reshape(n_seq, n_heads * hd)


def _out_proj_kernel(osb_ref, ocm_ref, h_ref, gsb_ref, w_ref, gffn_ref, h1_ref, f_ref, o_scr):
    sbw = osb_ref.shape[1]
    o_scr[:, :sbw] = (_rms_rows(osb_ref[...]) * gsb_ref[...]).astype(BF16)
    o_scr[:, sbw:] = ocm_ref[...]
    h1 = h_ref[...] + jnp.dot(o_scr[...], w_ref[...], preferred_element_type=F32)
    h1_ref[...] = h1
    f_ref[...] = (_rms_rows(h1) * gffn_ref[...]).astype(BF16)


def _out_proj(o_sb, o_cm_n, h, g_out_sb, w_out_b, g_ffn, *, tm):
    T, d_model = h.shape
    sbw, cmw = o_sb.shape[1], o_cm_n.shape[1]
    assert T % tm == 0 and w_out_b.shape == (sbw + cmw, d_model)
    row_blk = lambda i: (i, 0)
    const2 = lambda i: (0, 0)
    return pl.pallas_call(
        _out_proj_kernel,
        grid=(T // tm,),
        in_specs=[pl.BlockSpec((tm, sbw), row_blk),
                  pl.BlockSpec((tm, cmw), row_blk),
                  pl.BlockSpec((tm, d_model), row_blk),
                  pl.BlockSpec((1, sbw), const2),
                  _resident((sbw + cmw, d_model)),
                  pl.BlockSpec((1, d_model), const2)],
        out_specs=[pl.BlockSpec((tm, d_model), row_blk), pl.BlockSpec((tm, d_model), row_blk)],
        out_shape=[jax.ShapeDtypeStruct((T, d_model), F32), jax.ShapeDtypeStruct((T, d_model), BF16)],
        scratch_shapes=[pltpu.VMEM((tm, sbw + cmw), BF16)],
        compiler_params=_cparams("arbitrary"),
        name="out_proj",
    )(o_sb, o_cm_n, h, g_out_sb, w_out_b, g_ffn)


def _ffn_kernel(*refs, side_pages, pages_per_seq, total_pages):
    if side_pages:
        slot_page_ref, f_ref, wg_ref, wu_ref, wd_ref, q_ref, bias_ref = refs[:7]
        k_refs = refs[7:7 + side_pages]
        v_refs = refs[7 + side_pages:7 + 2 * side_pages]
        delta_ref, pacc_ref, ptot_ref = refs[7 + 2 * side_pages:]
    else:
        f_ref, wg_ref, wu_ref, wd_ref, delta_ref = refs
    tf = wd_ref.shape[0]
    i, j = pl.program_id(0), pl.program_id(1)

    @pl.when(j == 0)
    def _():
        delta_ref[...] = jnp.zeros_like(delta_ref)

    gate_up = jnp.dot(f_ref[...], jnp.concatenate([wg_ref[...], wu_ref[...]], axis=1),
                      preferred_element_type=F32)
    gate, up = gate_up[:, :tf], gate_up[:, tf:]
    if side_pages:
        first = (i * pl.num_programs(1) + j) * side_pages
        bias2 = bias_ref[...] * LOG2E
        weights = []
        for r in range(side_pages):
            seq = jnp.minimum(first + r, total_pages - 1) // pages_per_seq
            w, total = _page_weights(q_ref[seq].astype(BF16), bias2, k_refs[r])
            ptot_ref[r] = jnp.broadcast_to(total, ptot_ref.shape[1:])
            weights.append(w)
    act = (gate * jax.nn.sigmoid(gate) * up).astype(BF16)
    delta_ref[...] += jnp.dot(act, wd_ref[...], preferred_element_type=F32)
    if side_pages:
        for r in range(side_pages):
            pacc_ref[r] = _page_output(weights[r], v_refs[r])


def _ffn(f, wg_b, wu_b, wd_b, *, tm, tf, side=None, side_pages=0):
    T, d_model = f.shape
    d_ff = wd_b.shape[0]
    assert T % tm == 0 and d_ff % tf == 0
    grid = (T // tm, d_ff // tf)
    in_specs = [pl.BlockSpec((tm, d_model), lambda i, j, *_: (i, 0),
                             pipeline_mode=pl.Buffered(1) if side is not None else None),
                pl.BlockSpec((d_model, tf), lambda i, j, *_: (0, j)),
                pl.BlockSpec((d_model, tf), lambda i, j, *_: (0, j)),
                pl.BlockSpec((tf, d_model), lambda i, j, *_: (j, 0))]
    out_specs = [pl.BlockSpec((tm, d_model), lambda i, j, *_: (i, 0))]
    out_shape = [jax.ShapeDtypeStruct((T, d_model), F32)]
    args = [f, wg_b, wu_b, wd_b]
    if side is None:
        return pl.pallas_call(
            functools.partial(_ffn_kernel, side_pages=0, pages_per_seq=0, total_pages=0),
            grid=grid, in_specs=in_specs, out_specs=out_specs, out_shape=out_shape,
            compiler_params=_cparams("arbitrary", "arbitrary"), name="ffn",
        )(*args)[0]

    q, sb_bias, cache_k, cache_v, layer, page_table = side
    n_seq, pages_per_seq = page_table.shape
    _, _, page, n_heads, hd = cache_k.shape
    n = page * n_heads
    total_pages = n_seq * pages_per_seq
    n_steps = grid[0] * grid[1]
    assert n_heads == SUBLANES and hd == HEAD_DIM and n & (n - 1) == 0
    assert n_steps * side_pages >= total_pages

    slot_page = jnp.pad(page_table.reshape(-1), (0, n_steps * side_pages - total_pages), mode="edge")

    def page_spec(r):
        return pl.BlockSpec((None, None, page, n_heads, hd),
                            lambda i, j, sp: (layer, sp[(i * grid[1] + j) * side_pages + r], 0, 0, 0))

    step_pages = lambda: pl.BlockSpec((side_pages, n_heads, hd), lambda i, j, pt: (i * grid[1] + j, 0, 0))
    part = jax.ShapeDtypeStruct((n_steps * side_pages, n_heads, hd), F32)
    delta, pacc, ptot = pl.pallas_call(
        functools.partial(_ffn_kernel, side_pages=side_pages, pages_per_seq=pages_per_seq,
                          total_pages=total_pages),
        grid_spec=pltpu.PrefetchScalarGridSpec(
            num_scalar_prefetch=1,
            grid=grid,
            in_specs=in_specs
                     + [pl.BlockSpec((n_seq, n_heads, hd), lambda i, j, pt: (0, 0, 0)),
                        pl.BlockSpec((n_heads, n), lambda i, j, pt: (0, 0))]
                     + [page_spec(r) for r in range(side_pages)] * 2,
            out_specs=out_specs + [step_pages(), step_pages()],
        ),
        out_shape=out_shape + [part, part],
        compiler_params=_cparams("arbitrary", "arbitrary"),
        name="ffn_with_decode_pages",
    )(slot_page, *args, q, jnp.broadcast_to(sb_bias[:, None], (n_heads, n)),
      *([cache_k] * side_pages), *([cache_v] * side_pages))
    return delta, _decode_combine(pacc, ptot, n_seq, pages_per_seq)


def _ple_kernel(h1_ref, delta_ref, p_ref, gple_ref, wg_ref, wp_ref, o_ref):
    h2 = h1_ref[...] + delta_ref[...]
    hn = (_rms_rows(h2) * gple_ref[...]).astype(BF16)
    gate = jax.nn.sigmoid(jnp.dot(hn, wg_ref[...], preferred_element_type=F32))
    proj = jnp.dot(p_ref[...].astype(BF16), wp_ref[...], preferred_element_type=F32)
    o_ref[...] = h2 + gate * proj


def _ple(h1, delta, p, g_ple, wpg_b, wpp_b, *, tm):
    T, d_model = h1.shape
    ple_dim = p.shape[1]
    assert T % tm == 0
    row_blk = lambda i: (i, 0)
    const2 = lambda i: (0, 0)
    return pl.pallas_call(
        _ple_kernel,
        grid=(T // tm,),
        in_specs=[pl.BlockSpec((tm, d_model), row_blk),
                  pl.BlockSpec((tm, d_model), row_blk),
                  pl.BlockSpec((tm, ple_dim), row_blk),
                  pl.BlockSpec((1, d_model), const2),
                  _resident((d_model, d_model)),
                  _resident((ple_dim, d_model))],
        out_specs=pl.BlockSpec((tm, d_model), row_blk),
        out_shape=jax.ShapeDtypeStruct((T, d_model), F32),
        compiler_params=_cparams("arbitrary"),
        name="ple",
    )(h1, delta, p, g_ple, wpg_b, wpp_b)


PROMPT_FFN_ROWS = 1024
FFN_COLS = 256


def kernel(x_prompt, x_sample, cache_k, cache_v, page_table, p_prompt, p_sample, g_mix, w_in, g_q, g_k, sb_bias, sgu_ln_g, sgu_ln_b, w_s, b_s, g_out_sb, g_out_cm, w_out, g_ffn, w_gate, w_up, w_down, g_ple, w_ple_gate, w_ple_proj):
    depth = w_in.shape[0]
    B, L, d_model = x_prompt.shape
    Bs, Ls, _ = x_sample.shape
    n_sb = cache_k.shape[3]
    n_cm = w_s.shape[1]
    d_ff = w_gate.shape[2]
    sbw, cmw = n_sb * HEAD_DIM, n_cm * HEAD_DIM
    assert Ls == 1 and n_sb == n_cm and cache_k.shape[4] == HEAD_DIM
    assert (page_table.shape[1] * cache_k.shape[2]) % CHUNK == 0 and L % CHUNK == 0
    ffn_steps = (B * L // PROMPT_FFN_ROWS) * (d_ff // FFN_COLS)
    side_pages = -(-page_table.size // ffn_steps)

    hp = x_prompt.reshape(B * L, d_model)
    hs = x_sample.reshape(Bs * Ls, d_model)
    row = lambda a: a.reshape(1, -1)
    kp_l, vp_l, ks_l, vs_l, cs_l = [], [], [], [], []
    for i in range(depth):
        w_in_b = w_in[i].astype(BF16)
        proj_args = (row(g_mix[i]), w_in_b, row(g_q[i]), row(g_k[i]), row(sgu_ln_g[i]), row(sgu_ln_b[i]),
                     w_s[i], b_s[i].T, row(g_out_cm[i]))

        q_b, k_f, k_b, v_f, v_b, ocm_n = _in_proj(hp, *proj_args, tm=512, fresh_chunk_rows=False)
        q_s, ks_f, _, vs_f, _, ocm_s, cs = _in_proj(hs, *proj_args, tm=Bs, fresh_chunk_rows=True)

        o_sb, (wg_b, wu_b, wd_b, w_out_b, wpg_b, wpp_b) = _sb_prompt(
            q_b.reshape(B, L, sbw), k_b.reshape(B, L, sbw), v_b.reshape(B, L, sbw), sb_bias[i],
            to_cast=(w_gate[i], w_up[i], w_down[i], w_out[i], w_ple_gate[i], w_ple_proj[i]),
            blk=256, heads_per_step=8)
        out_args = (row(g_out_sb[i]), w_out_b, row(g_ffn[i]))
        ple_args = (row(g_ple[i]), wpg_b, wpp_b)
        h1, f = _out_proj(o_sb.reshape(B * L, sbw), ocm_n, hp, *out_args, tm=512)
        delta, o_sb_s = _ffn(f, wg_b, wu_b, wd_b, tm=PROMPT_FFN_ROWS, tf=FFN_COLS, side_pages=side_pages,
                             side=(q_s.reshape(Bs, n_sb, HEAD_DIM), sb_bias[i], cache_k, cache_v, i, page_table))
        hp = _ple(h1, delta, p_prompt[i].reshape(B * L, -1), *ple_args, tm=512)
        kp_l.append(k_f.reshape(B, L, n_sb, HEAD_DIM))
        vp_l.append(v_f.reshape(B, L, n_sb, HEAD_DIM))

        h1_s, f_s = _out_proj(o_sb_s, ocm_s, hs, *out_args, tm=Bs)
        delta_s = _ffn(f_s, wg_b, wu_b, wd_b, tm=Bs, tf=2 * FFN_COLS)
        hs = _ple(h1_s, delta_s, p_sample[i].reshape(Bs, -1), *ple_args, tm=Bs)
        ks_l.append(ks_f.reshape(Bs, Ls, n_sb, HEAD_DIM))
        vs_l.append(vs_f.reshape(Bs, Ls, n_sb, HEAD_DIM))
        cs_l.append(cs.reshape(Bs, Ls, n_cm, HEAD_DIM))

    return (hp.reshape(B, L, d_model), hs.reshape(Bs, Ls, d_model),
            jnp.stack(kp_l), jnp.stack(vp_l), jnp.stack(ks_l), jnp.stack(vs_l), jnp.stack(cs_l))
```

```python
import functools
import math

import jax
import jax.numpy as jnp
from jax import lax
from jax.experimental import pallas as pl
from jax.experimental.pallas import tpu as pltpu

HEAD_DIM = 128
CHUNK = 128
EPS = 1e-6
SB_SCALE = 1.0 / math.sqrt(HEAD_DIM)
INV_SQRT2 = 0.7071067811865476
LOG2E = 1.4426950408889634
Q_SCALE = SB_SCALE * LOG2E

VMEM_LIMIT = 60 * 1024 * 1024

F32 = jnp.float32
BF16 = jnp.bfloat16


def _cparams(*sem):
    return pltpu.CompilerParams(dimension_semantics=sem, vmem_limit_bytes=VMEM_LIMIT)


def _resident(shape):
    return pl.BlockSpec(shape, lambda i: (0,) * len(shape), pipeline_mode=pl.Buffered(1))


def _rms_rows(x):
    return x * lax.rsqrt(jnp.mean(x * x, axis=-1, keepdims=True) + EPS)


def _gelu(x):
    return 0.5 * x * (1.0 + lax.erf(x * INV_SQRT2))


def _log2_sigmoid_pair(z2):
    ls = jnp.minimum(z2, 0.0) - jnp.log2(1.0 + jnp.exp2(-jnp.abs(z2)))
    return ls, ls - z2


def _split_bf16(x):
    hi = x.astype(BF16)
    lo = (x - hi.astype(F32)).astype(BF16)
    return hi, lo


def _inproj_kernel(x_ref, gmix_ref, w_ref, gq_ref, gk_ref, lng_ref, lnb_ref, ws_ref, bs_ref, gcm_ref,
                   q_ref, kf_ref, kb_ref, vf_ref, vb_ref, ocm_ref, *rest, n_heads, fresh_chunk_rows):
    cs_ref = rest[0] if fresh_chunk_rows else None
    xn_ref, u_ref = rest[-2:]
    xn_ref[...] = (_rms_rows(x_ref[...]) * gmix_ref[...]).astype(BF16)

    tm = xn_ref.shape[0]
    heads_per_dot = 4

    def heads_of(group):
        slab = heads_per_dot * HEAD_DIM
        for p in range(n_heads // heads_per_dot):
            start = group * n_heads * HEAD_DIM + p * slab
            zp = jnp.dot(xn_ref[...], w_ref[:, start:start + slab], preferred_element_type=F32)
            for hh in range(heads_per_dot):
                yield heads_per_dot * p + hh, zp[:, hh * HEAD_DIM:(hh + 1) * HEAD_DIM]

    def cols(h):
        return slice(h * HEAD_DIM, (h + 1) * HEAD_DIM)

    for h, z in heads_of(0):
        q = _rms_rows(z) * gq_ref[...]
        q_ref[:, cols(h)] = (q * Q_SCALE).astype(q_ref.dtype)

    for h, z in heads_of(1):
        k = _rms_rows(z) * gk_ref[...]
        kf_ref[:, cols(h)] = k
        kb_ref[:, cols(h)] = k.astype(BF16)

    for h, z in heads_of(2):
        vf_ref[:, cols(h)] = z
        vb_ref[:, cols(h)] = z.astype(BF16)

    for h, z in heads_of(3):
        u_ref[:, cols(h)] = _gelu(z)

    if not fresh_chunk_rows:
        row = lax.broadcasted_iota(jnp.int32, (CHUNK, CHUNK), 0)
        col = lax.broadcasted_iota(jnp.int32, (CHUNK, CHUNK), 1)
        tril = col <= row
    for h, z in heads_of(4):
        gh = _gelu(z)
        mu = jnp.mean(gh, axis=-1, keepdims=True)
        xc = gh - mu
        vc = xc * lax.rsqrt(jnp.mean(xc * xc, axis=-1, keepdims=True) + EPS)
        vc = vc * lng_ref[:, cols(h)] + lnb_ref[:, cols(h)]
        if fresh_chunk_rows:
            cs_ref[:, cols(h)] = vc
            mixed = ws_ref[h, 0:1, 0:1] * vc + bs_ref[0:1, h:h + 1]
            u_ref[:, cols(h)] = u_ref[:, cols(h)] * mixed
        else:
            wm = jnp.where(tril, ws_ref[h], 0.0).astype(BF16)
            for c in range(tm // CHUNK):
                rows = slice(c * CHUNK, (c + 1) * CHUNK)
                mixed = jnp.dot(wm, vc[rows].astype(BF16), preferred_element_type=F32)
                mixed = mixed + bs_ref[:, h:h + 1]
                u_ref[rows, cols(h)] = u_ref[rows, cols(h)] * mixed
    ocm_ref[...] = (_rms_rows(u_ref[...]) * gcm_ref[...]).astype(BF16)


def _in_proj(x, g_mix, w_in_b, g_q, g_k, ln_g, ln_b, w_s, b_s_t, g_out_cm, *, tm, fresh_chunk_rows):
    T, d_model = x.shape
    n_heads = w_s.shape[0]
    gw = n_heads * HEAD_DIM
    assert w_in_b.shape == (d_model, 5 * gw) and T % tm == 0
    assert fresh_chunk_rows or tm % CHUNK == 0
    row_blk = lambda i: (i, 0)
    const2 = lambda i: (0, 0)
    out_f32 = jax.ShapeDtypeStruct((T, gw), F32)
    out_b16 = jax.ShapeDtypeStruct((T, gw), BF16)
    rows = lambda: pl.BlockSpec((tm, gw), row_blk)
    out_shape = [out_f32 if fresh_chunk_rows else out_b16, out_f32, out_b16, out_f32, out_b16, out_b16]
    if fresh_chunk_rows:
        out_shape.append(out_f32)
    return pl.pallas_call(
        functools.partial(_inproj_kernel, n_heads=n_heads, fresh_chunk_rows=fresh_chunk_rows),
        grid=(T // tm,),
        in_specs=[
            pl.BlockSpec((tm, d_model), row_blk),
            pl.BlockSpec((1, d_model), const2),
            _resident((d_model, 5 * gw)),
            pl.BlockSpec((1, HEAD_DIM), const2),
            pl.BlockSpec((1, HEAD_DIM), const2),
            pl.BlockSpec((1, gw), const2),
            pl.BlockSpec((1, gw), const2),
            pl.BlockSpec((n_heads, CHUNK, CHUNK), lambda i: (0, 0, 0)),
            pl.BlockSpec((CHUNK, n_heads), const2),
            pl.BlockSpec((1, gw), const2),
        ],
        out_specs=[rows() for _ in out_shape],
        out_shape=out_shape,
        scratch_shapes=[pltpu.VMEM((tm, d_model), BF16), pltpu.VMEM((tm, gw), F32)],
        compiler_params=_cparams("arbitrary"),
        name="in_proj",
    )(x, g_mix, w_in_b, g_q, g_k, ln_g, ln_b, w_s, b_s_t, g_out_cm)


def _sb_prompt_kernel(bias_ref, q_ref, k_ref, v_ref, *rest, blk, heads_per_step):
    n_cast = (len(rest) - 1) // 2
    o_ref = rest[n_cast]
    for w_ref, wb_ref in zip(rest[:n_cast], rest[n_cast + 1:]):
        wb_ref[...] = w_ref[...].astype(BF16)
    hg = pl.program_id(1)
    i = pl.program_id(2)
    row = lax.broadcasted_iota(jnp.int32, (blk, blk), 0)
    col = lax.broadcasted_iota(jnp.int32, (blk, blk), 1)
    later = jnp.where(row > col, 1.0, 0.0).astype(BF16)
    causal = col < row
    heads = range(heads_per_step)
    cols = [slice(g * HEAD_DIM, (g + 1) * HEAD_DIM) for g in heads]

    def block(j, runs, on_diagonal):
        start = pl.multiple_of(j * blk, blk)
        zs = [lax.dot_general(q_ref[0, :, cols[g]], k_ref[0, pl.ds(start, blk), cols[g]],
                              (((1,), (1,)), ((), ())), preferred_element_type=F32)
              + bias_ref[hg * heads_per_step + g] * LOG2E for g in heads]
        lss, lks = [], []
        for g in heads:
            ls, lk = _log2_sigmoid_pair(zs[g])
            lss.append(ls)
            lks.append(jnp.where(causal, lk, 0.0) if on_diagonal else lk)
        afters = [jnp.dot(lk.astype(BF16), later, preferred_element_type=F32) for lk in lks]
        weights = []
        for g in heads:
            a = jnp.exp2(lss[g] + afters[g] + runs[g])
            weights.append((jnp.where(causal, a, 0.0) if on_diagonal else a).astype(BF16))
        for g in heads:
            pv = jnp.dot(weights[g], v_ref[0, pl.ds(start, blk), cols[g]], preferred_element_type=F32)
            if on_diagonal:
                o_ref[0, :, cols[g]] = pv
            else:
                o_ref[0, :, cols[g]] += pv
        return tuple(runs[g] + jnp.sum(lks[g], axis=-1, keepdims=True) for g in heads)

    runs = block(i, (jnp.zeros((blk, 1), F32),) * heads_per_step, True)
    lax.fori_loop(0, i, lambda t, runs: block(i - 1 - t, runs, False), runs)


BF16_SUBLANES = 16


def _sb_prompt(q_b, k_b, v_b, sb_bias, to_cast=(), *, blk, heads_per_step):
    B, L, sbw = q_b.shape
    n_heads = sbw // HEAD_DIM
    assert L % blk == 0 and n_heads % heads_per_step == 0
    gw = heads_per_step * HEAD_DIM
    grid = (B, n_heads // heads_per_step, L // blk)
    n_steps = grid[0] * grid[1] * grid[2]
    qo_spec = lambda: pl.BlockSpec((1, blk, gw), lambda b, h, i, bias: (b, i, h))
    kv_spec = lambda: pl.BlockSpec((1, L, gw), lambda b, h, i, bias: (b, 0, h))

    def slab_spec(w):
        rows = next(r for r in range(BF16_SUBLANES, w.shape[0] + 1, BF16_SUBLANES)
                    if w.shape[0] % r == 0 and r * n_steps >= w.shape[0])
        last = w.shape[0] // rows - 1
        return pl.BlockSpec(
            (rows, w.shape[1]),
            lambda b, h, i, bias: (jnp.minimum((b * grid[1] + h) * grid[2] + i, last), 0))

    out = pl.pallas_call(
        functools.partial(_sb_prompt_kernel, blk=blk, heads_per_step=heads_per_step),
        grid_spec=pltpu.PrefetchScalarGridSpec(
            num_scalar_prefetch=1,
            grid=grid,
            in_specs=[qo_spec(), kv_spec(), kv_spec()] + [slab_spec(w) for w in to_cast],
            out_specs=[qo_spec()] + [slab_spec(w) for w in to_cast],
        ),
        out_shape=[jax.ShapeDtypeStruct((B, L, sbw), F32)]
                  + [jax.ShapeDtypeStruct(w.shape, BF16) for w in to_cast],
        compiler_params=_cparams("arbitrary", "arbitrary", "arbitrary"),
        name="sb_prompt",
    )(sb_bias, q_b, k_b, v_b, *to_cast)
    return out[0], out[1:]


SUBLANES = 8


def _page_weights(qm, bias2, k_ref):
    page, n_heads, hd = k_ref.shape
    n = page * n_heads
    lane = lax.broadcasted_iota(jnp.int32, (n_heads, n), 1)
    sub = lax.broadcasted_iota(jnp.int32, (n_heads, n), 0)
    kp = k_ref[...].reshape(n, hd).astype(BF16)
    z = lax.dot_general(qm, kp, (((1,), (1,)), ((), ())), preferred_element_type=F32) + bias2
    ls, lk = _log2_sigmoid_pair(z)
    suffix = lk
    shift = n_heads
    while shift < n:
        ahead = pltpu.roll(suffix, n - shift, axis=1)
        suffix = suffix + jnp.where(lane < n - shift, ahead, 0.0)
        shift *= 2
    total = jnp.sum(jnp.where(lane == sub, suffix, 0.0), axis=1, keepdims=True)
    a = jnp.where(lane % n_heads == sub, jnp.exp2(ls + (suffix - lk)), 0.0)
    hi, lo = _split_bf16(a)
    return jnp.concatenate([hi, lo], axis=0), total


def _page_output(weights, v_ref):
    page, n_heads, hd = v_ref.shape
    o2 = jnp.dot(weights, v_ref[...].reshape(page * n_heads, hd).astype(BF16), preferred_element_type=F32)
    return o2[:n_heads] + o2[n_heads:]


COMBINE_SEQS = 8


def _decode_combine_kernel(pacc_ref, ptot_ref, o_ref, *, pages_per_seq):
    n_seq = o_ref.shape[0]

    def body(t, carry):
        out = []
        for s, (run, acc) in enumerate(carry):
            p = (s + 1) * pages_per_seq - 1 - t
            out.append((run + ptot_ref[p], acc + jnp.exp2(run) * pacc_ref[p]))
        return tuple(out)

    zero = jnp.zeros(o_ref.shape[1:], F32)
    final = lax.fori_loop(0, pages_per_seq, body, ((zero, zero),) * n_seq)
    for s in range(n_seq):
        o_ref[s] = final[s][1]


def _decode_combine(pacc, ptot, n_seq, pages_per_seq):
    _, n_heads, hd = pacc.shape
    group = math.gcd(n_seq, COMBINE_SEQS)
    seq_pages = lambda: pl.BlockSpec((group * pages_per_seq, n_heads, hd), lambda b: (b, 0, 0))
    out = pl.pallas_call(
        functools.partial(_decode_combine_kernel, pages_per_seq=pages_per_seq),
        grid=(n_seq // group,),
        in_specs=[seq_pages(), seq_pages()],
        out_specs=pl.BlockSpec((group, n_heads, hd), lambda b: (b, 0, 0)),
        out_shape=jax.ShapeDtypeStruct((n_seq, n_heads, hd), F32),
        compiler_params=_cparams("arbitrary"),
        name="sb_decode_combine",
    )(pacc, ptot)
    return out.reshape(n_seq, n_heads * hd)


def _out_proj_kernel(osb_ref, ocm_ref, h_ref, gsb_ref, w_ref, gffn_ref, h1_ref, f_ref, o_scr):
    sbw = osb_ref.shape[1]
    o_scr[:, :sbw] = (_rms_rows(osb_ref[...]) * gsb_ref[...]).astype(BF16)
    o_scr[:, sbw:] = ocm_ref[...]
    h1 = h_ref[...] + jnp.dot(o_scr[...], w_ref[...], preferred_element_type=F32)
    h1_ref[...] = h1
    f_ref[...] = (_rms_rows(h1) * gffn_ref[...]).astype(BF16)


def _out_proj(o_sb, o_cm_n, h, g_out_sb, w_out_b, g_ffn, *, tm):
    T, d_model = h.shape
    sbw, cmw = o_sb.shape[1], o_cm_n.shape[1]
    assert T % tm == 0 and w_out_b.shape == (sbw + cmw, d_model)
    row_blk = lambda i: (i, 0)
    const2 = lambda i: (0, 0)
    return pl.pallas_call(
        _out_proj_kernel,
        grid=(T // tm,),
        in_specs=[pl.BlockSpec((tm, sbw), row_blk),
                  pl.BlockSpec((tm, cmw), row_blk),
                  pl.BlockSpec((tm, d_model), row_blk),
                  pl.BlockSpec((1, sbw), const2),
                  _resident((sbw + cmw, d_model)),
                  pl.BlockSpec((1, d_model), const2)],
        out_specs=[pl.BlockSpec((tm, d_model), row_blk), pl.BlockSpec((tm, d_model), row_blk)],
        out_shape=[jax.ShapeDtypeStruct((T, d_model), F32), jax.ShapeDtypeStruct((T, d_model), BF16)],
        scratch_shapes=[pltpu.VMEM((tm, sbw + cmw), BF16)],
        compiler_params=_cparams("arbitrary"),
        name="out_proj",
    )(o_sb, o_cm_n, h, g_out_sb, w_out_b, g_ffn)


def _ffn_kernel(*refs, side_pages, pages_per_seq, total_pages):
    if side_pages:
        slot_page_ref, f_ref, wg_ref, wu_ref, wd_ref, q_ref, bias_ref = refs[:7]
        k_refs = refs[7:7 + side_pages]
        v_refs = refs[7 + side_pages:7 + 2 * side_pages]
        delta_ref, pacc_ref, ptot_ref = refs[7 + 2 * side_pages:]
    else:
        f_ref, wg_ref, wu_ref, wd_ref, delta_ref = refs
    tf = wd_ref.shape[0]
    i, j = pl.program_id(0), pl.program_id(1)

    @pl.when(j == 0)
    def _():
        delta_ref[...] = jnp.zeros_like(delta_ref)

    gate_up = jnp.dot(f_ref[...], jnp.concatenate([wg_ref[...], wu_ref[...]], axis=1),
                      preferred_element_type=F32)
    gate, up = gate_up[:, :tf], gate_up[:, tf:]
    if side_pages:
        first = (i * pl.num_programs(1) + j) * side_pages
        bias2 = bias_ref[...] * LOG2E
        weights = []
        for r in range(side_pages):
            seq = jnp.minimum(first + r, total_pages - 1) // pages_per_seq
            w, total = _page_weights(q_ref[seq].astype(BF16), bias2, k_refs[r])
            ptot_ref[r] = jnp.broadcast_to(total, ptot_ref.shape[1:])
            weights.append(w)
    act = (gate * jax.nn.sigmoid(gate) * up).astype(BF16)
    delta_ref[...] += jnp.dot(act, wd_ref[...], preferred_element_type=F32)
    if side_pages:
        for r in range(side_pages):
            pacc_ref[r] = _page_output(weights[r], v_refs[r])


def _ffn(f, wg_b, wu_b, wd_b, *, tm, tf, side=None, side_pages=0):
    T, d_model = f.shape
    d_ff = wd_b.shape[0]
    assert T % tm == 0 and d_ff % tf == 0
    grid = (T // tm, d_ff // tf)
    in_specs = [pl.BlockSpec((tm, d_model), lambda i, j, *_: (i, 0)),
                pl.BlockSpec((d_model, tf), lambda i, j, *_: (0, j)),
                pl.BlockSpec((d_model, tf), lambda i, j, *_: (0, j)),
                pl.BlockSpec((tf, d_model), lambda i, j, *_: (j, 0))]
    out_specs = [pl.BlockSpec((tm, d_model), lambda i, j, *_: (i, 0))]
    out_shape = [jax.ShapeDtypeStruct((T, d_model), F32)]
    args = [f, wg_b, wu_b, wd_b]
    if side is None:
        return pl.pallas_call(
            functools.partial(_ffn_kernel, side_pages=0, pages_per_seq=0, total_pages=0),
            grid=grid, in_specs=in_specs, out_specs=out_specs, out_shape=out_shape,
            compiler_params=_cparams("arbitrary", "arbitrary"), name="ffn",
        )(*args)[0]

    q, sb_bias, cache_k, cache_v, layer, page_table = side
    n_seq, pages_per_seq = page_table.shape
    _, _, page, n_heads, hd = cache_k.shape
    n = page * n_heads
    total_pages = n_seq * pages_per_seq
    n_steps = grid[0] * grid[1]
    assert n_heads == SUBLANES and hd == HEAD_DIM and n & (n - 1) == 0
    assert n_steps * side_pages >= total_pages

    slot_page = jnp.pad(page_table.reshape(-1), (0, n_steps * side_pages - total_pages), mode="edge")

    def page_spec(r):
        return pl.BlockSpec((None, None, page, n_heads, hd),
                            lambda i, j, sp: (layer, sp[(i * grid[1] + j) * side_pages + r], 0, 0, 0))

    step_pages = lambda: pl.BlockSpec((side_pages, n_heads, hd), lambda i, j, pt: (i * grid[1] + j, 0, 0))
    part = jax.ShapeDtypeStruct((n_steps * side_pages, n_heads, hd), F32)
    delta, pacc, ptot = pl.pallas_call(
        functools.partial(_ffn_kernel, side_pages=side_pages, pages_per_seq=pages_per_seq,
                          total_pages=total_pages),
        grid_spec=pltpu.PrefetchScalarGridSpec(
            num_scalar_prefetch=1,
            grid=grid,
            in_specs=in_specs
                     + [pl.BlockSpec((n_seq, n_heads, hd), lambda i, j, pt: (0, 0, 0)),
                        pl.BlockSpec((n_heads, n), lambda i, j, pt: (0, 0))]
                     + [page_spec(r) for r in range(side_pages)] * 2,
            out_specs=out_specs + [step_pages(), step_pages()],
        ),
        out_shape=out_shape + [part, part],
        compiler_params=_cparams("arbitrary", "arbitrary"),
        name="ffn_with_decode_pages",
    )(slot_page, *args, q, jnp.broadcast_to(sb_bias[:, None], (n_heads, n)),
      *([cache_k] * side_pages), *([cache_v] * side_pages))
    return delta, _decode_combine(pacc, ptot, n_seq, pages_per_seq)


def _ple_kernel(h1_ref, delta_ref, p_ref, gple_ref, wg_ref, wp_ref, o_ref):
    h2 = h1_ref[...] + delta_ref[...]
    hn = (_rms_rows(h2) * gple_ref[...]).astype(BF16)
    gate = jax.nn.sigmoid(jnp.dot(hn, wg_ref[...], preferred_element_type=F32))
    proj = jnp.dot(p_ref[...].astype(BF16), wp_ref[...], preferred_element_type=F32)
    o_ref[...] = h2 + gate * proj


def _ple(h1, delta, p, g_ple, wpg_b, wpp_b, *, tm):
    T, d_model = h1.shape
    ple_dim = p.shape[1]
    assert T % tm == 0
    row_blk = lambda i: (i, 0)
    const2 = lambda i: (0, 0)
    return pl.pallas_call(
        _ple_kernel,
        grid=(T // tm,),
        in_specs=[pl.BlockSpec((tm, d_model), row_blk),
                  pl.BlockSpec((tm, d_model), row_blk),
                  pl.BlockSpec((tm, ple_dim), row_blk),
                  pl.BlockSpec((1, d_model), const2),
                  _resident((d_model, d_model)),
                  _resident((ple_dim, d_model))],
        out_specs=pl.BlockSpec((tm, d_model), row_blk),
        out_shape=jax.ShapeDtypeStruct((T, d_model), F32),
        compiler_params=_cparams("arbitrary"),
        name="ple",
    )(h1, delta, p, g_ple, wpg_b, wpp_b)


PROMPT_FFN_ROWS = 1024
FFN_COLS = 256


def kernel(x_prompt, x_sample, cache_k, cache_v, page_table, p_prompt, p_sample, g_mix, w_in, g_q, g_k, sb_bias, sgu_ln_g, sgu_ln_b, w_s, b_s, g_out_sb, g_out_cm, w_out, g_ffn, w_gate, w_up, w_down, g_ple, w_ple_gate, w_ple_proj):
    depth = w_in.shape[0]
    B, L, d_model = x_prompt.shape
    Bs, Ls, _ = x_sample.shape
    n_sb = cache_k.shape[3]
    n_cm = w_s.shape[1]
    d_ff = w_gate.shape[2]
    sbw, cmw = n_sb * HEAD_DIM, n_cm * HEAD_DIM
    assert Ls == 1 and n_sb == n_cm and cache_k.shape[4] == HEAD_DIM
    assert (page_table.shape[1] * cache_k.shape[2]) % CHUNK == 0 and L % CHUNK == 0
    ffn_steps = (B * L // PROMPT_FFN_ROWS) * (d_ff // FFN_COLS)
    side_pages = -(-page_table.size // ffn_steps)

    hp = x_prompt.reshape(B * L, d_model)
    hs = x_sample.reshape(Bs * Ls, d_model)
    row = lambda a: a.reshape(1, -1)
    kp_l, vp_l, ks_l, vs_l, cs_l = [], [], [], [], []
    for i in range(depth):
        w_in_b = w_in[i].astype(BF16)
        proj_args = (row(g_mix[i]), w_in_b, row(g_q[i]), row(g_k[i]), row(sgu_ln_g[i]), row(sgu_ln_b[i]),
                     w_s[i], b_s[i].T, row(g_out_cm[i]))

        q_b, k_f, k_b, v_f, v_b, ocm_n = _in_proj(hp, *proj_args, tm=512, fresh_chunk_rows=False)
        q_s, ks_f, _, vs_f, _, ocm_s, cs = _in_proj(hs, *proj_args, tm=Bs, fresh_chunk_rows=True)

        o_sb, (wg_b, wu_b, wd_b, w_out_b, wpg_b, wpp_b) = _sb_prompt(
            q_b.reshape(B, L, sbw), k_b.reshape(B, L, sbw), v_b.reshape(B, L, sbw), sb_bias[i],
            to_cast=(w_gate[i], w_up[i], w_down[i], w_out[i], w_ple_gate[i], w_ple_proj[i]),
            blk=256, heads_per_step=8)
        out_args = (row(g_out_sb[i]), w_out_b, row(g_ffn[i]))
        ple_args = (row(g_ple[i]), wpg_b, wpp_b)
        h1, f = _out_proj(o_sb.reshape(B * L, sbw), ocm_n, hp, *out_args, tm=512)
        delta, o_sb_s = _ffn(f, wg_b, wu_b, wd_b, tm=PROMPT_FFN_ROWS, tf=FFN_COLS, side_pages=side_pages,
                             side=(q_s.reshape(Bs, n_sb, HEAD_DIM), sb_bias[i], cache_k, cache_v, i, page_table))
        hp = _ple(h1, delta, p_prompt[i].reshape(B * L, -1), *ple_args, tm=512)
        kp_l.append(k_f.reshape(B, L, n_sb, HEAD_DIM))
        vp_l.append(v_f.reshape(B, L, n_sb, HEAD_DIM))

        h1_s, f_s = _out_proj(o_sb_s, ocm_s, hs, *out_args, tm=Bs)
        delta_s = _ffn(f_s, wg_b, wu_b, wd_b, tm=Bs, tf=2 * FFN_COLS)
        hs = _ple(h1_s, delta_s, p_sample[i].reshape(Bs, -1), *ple_args, tm=Bs)
        ks_l.append(ks_f.reshape(Bs, Ls, n_sb, HEAD_DIM))
        vs_l.append(vs_f.reshape(Bs, Ls, n_sb, HEAD_DIM))
        cs_l.append(cs.reshape(Bs, Ls, n_cm, HEAD_DIM))

    return (hp.reshape(B, L, d_model), hs.reshape(Bs, Ls, d_model),
            jnp.stack(kp_l), jnp.stack(vp_l), jnp.stack(ks_l), jnp.stack(vs_l), jnp.stack(cs_l))
```

```python
import functools
import math

import jax
import jax.numpy as jnp
from jax import lax
from jax.experimental import pallas as pl
from jax.experimental.pallas import tpu as pltpu

HEAD_DIM = 128
CHUNK = 128
EPS = 1e-6
SB_SCALE = 1.0 / math.sqrt(HEAD_DIM)
INV_SQRT2 = 0.7071067811865476
LOG2E = 1.4426950408889634
Q_SCALE = SB_SCALE * LOG2E

VMEM_LIMIT = 60 * 1024 * 1024

F32 = jnp.float32
BF16 = jnp.bfloat16


def _cparams(*sem):
    return pltpu.CompilerParams(dimension_semantics=sem, vmem_limit_bytes=VMEM_LIMIT)


def _resident(shape):
    return pl.BlockSpec(shape, lambda i: (0,) * len(shape), pipeline_mode=pl.Buffered(1))


def _rms_rows(x):
    return x * lax.rsqrt(jnp.mean(x * x, axis=-1, keepdims=True) + EPS)


def _gelu(x):
    return 0.5 * x * (1.0 + lax.erf(x * INV_SQRT2))


def _log2_sigmoid_pair(z2):
    ls = jnp.minimum(z2, 0.0) - jnp.log2(1.0 + jnp.exp2(-jnp.abs(z2)))
    return ls, ls - z2


def _split_bf16(x):
    hi = x.astype(BF16)
    lo = (x - hi.astype(F32)).astype(BF16)
    return hi, lo


def _inproj_kernel(x_ref, gmix_ref, w_ref, gq_ref, gk_ref, lng_ref, lnb_ref, ws_ref, bs_ref, gcm_ref,
                   q_ref, kf_ref, kb_ref, vf_ref, vb_ref, ocm_ref, *rest, n_heads, fresh_chunk_rows):
    cs_ref = rest[0] if fresh_chunk_rows else None
    xn_ref, u_ref = rest[-2:]
    xn_ref[...] = (_rms_rows(x_ref[...]) * gmix_ref[...]).astype(BF16)

    tm = xn_ref.shape[0]
    heads_per_dot = 4

    def heads_of(group):
        slab = heads_per_dot * HEAD_DIM
        for p in range(n_heads // heads_per_dot):
            start = group * n_heads * HEAD_DIM + p * slab
            zp = jnp.dot(xn_ref[...], w_ref[:, start:start + slab], preferred_element_type=F32)
            for hh in range(heads_per_dot):
                yield heads_per_dot * p + hh, zp[:, hh * HEAD_DIM:(hh + 1) * HEAD_DIM]

    def cols(h):
        return slice(h * HEAD_DIM, (h + 1) * HEAD_DIM)

    for h, z in heads_of(0):
        q = _rms_rows(z) * gq_ref[...]
        q_ref[:, cols(h)] = (q * Q_SCALE).astype(q_ref.dtype)

    for h, z in heads_of(1):
        k = _rms_rows(z) * gk_ref[...]
        kf_ref[:, cols(h)] = k
        kb_ref[:, cols(h)] = k.astype(BF16)

    for h, z in heads_of(2):
        vf_ref[:, cols(h)] = z
        vb_ref[:, cols(h)] = z.astype(BF16)

    for h, z in heads_of(3):
        u_ref[:, cols(h)] = _gelu(z)

    if not fresh_chunk_rows:
        row = lax.broadcasted_iota(jnp.int32, (CHUNK, CHUNK), 0)
        col = lax.broadcasted_iota(jnp.int32, (CHUNK, CHUNK), 1)
        tril = col <= row
    for h, z in heads_of(4):
        gh = _gelu(z)
        mu = jnp.mean(gh, axis=-1, keepdims=True)
        xc = gh - mu
        vc = xc * lax.rsqrt(jnp.mean(xc * xc, axis=-1, keepdims=True) + EPS)
        vc = vc * lng_ref[:, cols(h)] + lnb_ref[:, cols(h)]
        if fresh_chunk_rows:
            cs_ref[:, cols(h)] = vc
            mixed = ws_ref[h, 0:1, 0:1] * vc + bs_ref[0:1, h:h + 1]
            u_ref[:, cols(h)] = u_ref[:, cols(h)] * mixed
        else:
            wm = jnp.where(tril, ws_ref[h], 0.0).astype(BF16)
            for c in range(tm // CHUNK):
                rows = slice(c * CHUNK, (c + 1) * CHUNK)
                mixed = jnp.dot(wm, vc[rows].astype(BF16), preferred_element_type=F32)
                mixed = mixed + bs_ref[:, h:h + 1]
                u_ref[rows, cols(h)] = u_ref[rows, cols(h)] * mixed
    ocm_ref[...] = (_rms_rows(u_ref[...]) * gcm_ref[...]).astype(BF16)


def _in_proj(x, g_mix, w_in_b, g_q, g_k, ln_g, ln_b, w_s, b_s_t, g_out_cm, *, tm, fresh_chunk_rows):
    T, d_model = x.shape
    n_heads = w_s.shape[0]
    gw = n_heads * HEAD_DIM
    assert w_in_b.shape == (d_model, 5 * gw) and T % tm == 0
    assert fresh_chunk_rows or tm % CHUNK == 0
    row_blk = lambda i: (i, 0)
    const2 = lambda i: (0, 0)
    out_f32 = jax.ShapeDtypeStruct((T, gw), F32)
    out_b16 = jax.ShapeDtypeStruct((T, gw), BF16)
    rows = lambda: pl.BlockSpec((tm, gw), row_blk)
    out_shape = [out_f32 if fresh_chunk_rows else out_b16, out_f32, out_b16, out_f32, out_b16, out_b16]
    if fresh_chunk_rows:
        out_shape.append(out_f32)
    return pl.pallas_call(
        functools.partial(_inproj_kernel, n_heads=n_heads, fresh_chunk_rows=fresh_chunk_rows),
        grid=(T // tm,),
        in_specs=[
            pl.BlockSpec((tm, d_model), row_blk),
            pl.BlockSpec((1, d_model), const2),
            _resident((d_model, 5 * gw)),
            pl.BlockSpec((1, HEAD_DIM), const2),
            pl.BlockSpec((1, HEAD_DIM), const2),
            pl.BlockSpec((1, gw), const2),
            pl.BlockSpec((1, gw), const2),
            pl.BlockSpec((n_heads, CHUNK, CHUNK), lambda i: (0, 0, 0)),
            pl.BlockSpec((CHUNK, n_heads), const2),
            pl.BlockSpec((1, gw), const2),
        ],
        out_specs=[rows() for _ in out_shape],
        out_shape=out_shape,
        scratch_shapes=[pltpu.VMEM((tm, d_model), BF16), pltpu.VMEM((tm, gw), F32)],
        compiler_params=_cparams("arbitrary"),
        name="in_proj",
    )(x, g_mix, w_in_b, g_q, g_k, ln_g, ln_b, w_s, b_s_t, g_out_cm)


def _sb_prompt_kernel(bias_ref, q_ref, k_ref, v_ref, *rest, blk, heads_per_step):
    n_cast = (len(rest) - 1) // 2
    o_ref = rest[n_cast]
    for w_ref, wb_ref in zip(rest[:n_cast], rest[n_cast + 1:]):
        wb_ref[...] = w_ref[...].astype(BF16)
    hg = pl.program_id(1)
    i = pl.program_id(2)
    row = lax.broadcasted_iota(jnp.int32, (blk, blk), 0)
    col = lax.broadcasted_iota(jnp.int32, (blk, blk), 1)
    later = jnp.where(row > col, 1.0, 0.0).astype(BF16)
    causal = col < row
    heads = range(heads_per_step)
    cols = [slice(g * HEAD_DIM, (g + 1) * HEAD_DIM) for g in heads]

    def block(j, runs, on_diagonal):
        start = pl.multiple_of(j * blk, blk)
        zs = [lax.dot_general(q_ref[0, :, cols[g]], k_ref[0, pl.ds(start, blk), cols[g]],
                              (((1,), (1,)), ((), ())), preferred_element_type=F32)
              + bias_ref[hg * heads_per_step + g] * LOG2E for g in heads]
        lss, lks = [], []
        for g in heads:
            ls, lk = _log2_sigmoid_pair(zs[g])
            lss.append(ls)
            lks.append(jnp.where(causal, lk, 0.0) if on_diagonal else lk)
        afters = [jnp.dot(lk.astype(BF16), later, preferred_element_type=F32) for lk in lks]
        weights = []
        for g in heads:
            a = jnp.exp2(lss[g] + afters[g] + runs[g])
            weights.append((jnp.where(causal, a, 0.0) if on_diagonal else a).astype(BF16))
        for g in heads:
            pv = jnp.dot(weights[g], v_ref[0, pl.ds(start, blk), cols[g]], preferred_element_type=F32)
            if on_diagonal:
                o_ref[0, :, cols[g]] = pv
            else:
                o_ref[0, :, cols[g]] += pv
        return tuple(runs[g] + jnp.sum(lks[g], axis=-1, keepdims=True) for g in heads)

    runs = block(i, (jnp.zeros((blk, 1), F32),) * heads_per_step, True)

    def blocks_from(j, count, runs):
        for c in range(count):
            runs = block(j - c, runs, False)
        return runs

    runs = lax.fori_loop(0, i // 4, lambda t, runs: blocks_from(i - 1 - 4 * t, 4, runs), runs)
    left = i % 4
    runs = lax.cond(left >= 2, lambda runs: blocks_from(left - 1, 2, runs), lambda runs: runs, runs)

    @pl.when(left % 2 == 1)
    def _():
        block(0, runs, False)


BF16_SUBLANES = 16


def _sb_prompt(q_b, k_b, v_b, sb_bias, to_cast=(), *, blk, heads_per_step):
    B, L, sbw = q_b.shape
    n_heads = sbw // HEAD_DIM
    assert L % blk == 0 and n_heads % heads_per_step == 0
    gw = heads_per_step * HEAD_DIM
    grid = (B, n_heads // heads_per_step, L // blk)
    n_steps = grid[0] * grid[1] * grid[2]
    qo_spec = lambda: pl.BlockSpec((1, blk, gw), lambda b, h, i, bias: (b, i, h))
    kv_spec = lambda: pl.BlockSpec((1, L, gw), lambda b, h, i, bias: (b, 0, h), pipeline_mode=pl.Buffered(1))

    def slab_spec(w):
        rows = next(r for r in range(BF16_SUBLANES, w.shape[0] + 1, BF16_SUBLANES)
                    if w.shape[0] % r == 0 and r * n_steps >= w.shape[0])
        last = w.shape[0] // rows - 1
        return pl.BlockSpec(
            (rows, w.shape[1]),
            lambda b, h, i, bias: (jnp.minimum((b * grid[1] + h) * grid[2] + i, last), 0))

    out = pl.pallas_call(
        functools.partial(_sb_prompt_kernel, blk=blk, heads_per_step=heads_per_step),
        grid_spec=pltpu.PrefetchScalarGridSpec(
            num_scalar_prefetch=1,
            grid=grid,
            in_specs=[qo_spec(), kv_spec(), kv_spec()] + [slab_spec(w) for w in to_cast],
            out_specs=[qo_spec()] + [slab_spec(w) for w in to_cast],
        ),
        out_shape=[jax.ShapeDtypeStruct((B, L, sbw), F32)]
                  + [jax.ShapeDtypeStruct(w.shape, BF16) for w in to_cast],
        compiler_params=_cparams("arbitrary", "arbitrary", "arbitrary"),
        name="sb_prompt",
    )(sb_bias, q_b, k_b, v_b, *to_cast)
    return out[0], out[1:]


SUBLANES = 8


def _page_weights(qm, bias2, k_ref):
    page, n_heads, hd = k_ref.shape
    n = page * n_heads
    lane = lax.broadcasted_iota(jnp.int32, (n_heads, n), 1)
    sub = lax.broadcasted_iota(jnp.int32, (n_heads, n), 0)
    kp = k_ref[...].reshape(n, hd).astype(BF16)
    z = lax.dot_general(qm, kp, (((1,), (1,)), ((), ())), preferred_element_type=F32) + bias2
    ls, lk = _log2_sigmoid_pair(z)
    suffix = lk
    shift = n_heads
    while shift < n:
        ahead = pltpu.roll(suffix, n - shift, axis=1)
        suffix = suffix + jnp.where(lane < n - shift, ahead, 0.0)
        shift *= 2
    total = jnp.sum(jnp.where(lane == sub, suffix, 0.0), axis=1, keepdims=True)
    a = jnp.where(lane % n_heads == sub, jnp.exp2(ls + (suffix - lk)), 0.0)
    hi, lo = _split_bf16(a)
    return jnp.concatenate([hi, lo], axis=0), total


def _page_output(weights, v_ref):
    page, n_heads, hd = v_ref.shape
    o2 = jnp.dot(weights, v_ref[...].reshape(page * n_heads, hd).astype(BF16), preferred_element_type=F32)
    return o2[:n_heads] + o2[n_heads:]


COMBINE_SEQS = 8


def _decode_combine_kernel(pacc_ref, ptot_ref, q_ref, kn_ref, vn_ref, bias_ref, o_ref, *,
                           pages_per_seq, new_key_visible):
    n_seq = o_ref.shape[0]
    start = []
    for s in range(n_seq):
        z = jnp.sum(q_ref[s] * kn_ref[s], axis=-1, keepdims=True) + bias_ref[...] * LOG2E
        ls, lk = _log2_sigmoid_pair(z)
        start.append((jnp.where(new_key_visible, lk, 0.0),
                      jnp.where(new_key_visible, jnp.exp2(ls), 0.0) * vn_ref[s]))

    def body(t, carry):
        out = []
        for s, (run, acc) in enumerate(carry):
            p = (s + 1) * pages_per_seq - 1 - t
            out.append((run + ptot_ref[p], acc + jnp.exp2(run) * pacc_ref[p]))
        return tuple(out)

    final = lax.fori_loop(0, pages_per_seq, body, tuple(start))
    for s in range(n_seq):
        o_ref[s] = final[s][1]


def _decode_combine(pacc, ptot, q, k_new, v_new, sb_bias, pages_per_seq, new_key_visible):
    n_seq, n_heads, hd = q.shape
    group = math.gcd(n_seq, COMBINE_SEQS)
    seq_pages = lambda: pl.BlockSpec((group * pages_per_seq, n_heads, hd), lambda b: (b, 0, 0))
    seq_rows = lambda: pl.BlockSpec((group, n_heads, hd), lambda b: (b, 0, 0))
    out = pl.pallas_call(
        functools.partial(_decode_combine_kernel, pages_per_seq=pages_per_seq,
                          new_key_visible=new_key_visible),
        grid=(n_seq // group,),
        in_specs=[seq_pages(), seq_pages(), seq_rows(), seq_rows(), seq_rows(),
                  pl.BlockSpec((n_heads, hd), lambda b: (0, 0))],
        out_specs=seq_rows(),
        out_shape=jax.ShapeDtypeStruct((n_seq, n_heads, hd), F32),
        compiler_params=_cparams("arbitrary"),
        name="sb_decode_combine",
    )(pacc, ptot, q, k_new, v_new, jnp.broadcast_to(sb_bias[:, None], (n_heads, hd)))
    return out.reshape(n_seq, n_heads * hd)


def _out_proj_kernel(osb_ref, ocm_ref, h_ref, gsb_ref, w_ref, gffn_ref, h1_ref, f_ref, o_scr):
    sbw = osb_ref.shape[1]
    o_scr[:, :sbw] = (_rms_rows(osb_ref[...]) * gsb_ref[...]).astype(BF16)
    o_scr[:, sbw:] = ocm_ref[...]
    h1 = h_ref[...] + jnp.dot(o_scr[...], w_ref[...], preferred_element_type=F32)
    h1_ref[...] = h1
    f_ref[...] = (_rms_rows(h1) * gffn_ref[...]).astype(BF16)


def _out_proj(o_sb, o_cm_n, h, g_out_sb, w_out_b, g_ffn, *, tm):
    T, d_model = h.shape
    sbw, cmw = o_sb.shape[1], o_cm_n.shape[1]
    assert T % tm == 0 and w_out_b.shape == (sbw + cmw, d_model)
    row_blk = lambda i: (i, 0)
    const2 = lambda i: (0, 0)
    return pl.pallas_call(
        _out_proj_kernel,
        grid=(T // tm,),
        in_specs=[pl.BlockSpec((tm, sbw), row_blk),
                  pl.BlockSpec((tm, cmw), row_blk),
                  pl.BlockSpec((tm, d_model), row_blk),
                  pl.BlockSpec((1, sbw), const2),
                  _resident((sbw + cmw, d_model)),
                  pl.BlockSpec((1, d_model), const2)],
        out_specs=[pl.BlockSpec((tm, d_model), row_blk), pl.BlockSpec((tm, d_model), row_blk)],
        out_shape=[jax.ShapeDtypeStruct((T, d_model), F32), jax.ShapeDtypeStruct((T, d_model), BF16)],
        scratch_shapes=[pltpu.VMEM((tm, sbw + cmw), BF16)],
        compiler_params=_cparams("arbitrary"),
        name="out_proj",
    )(o_sb, o_cm_n, h, g_out_sb, w_out_b, g_ffn)


def _ffn_kernel(*refs, side_pages, pages_per_seq, total_pages):
    if side_pages:
        slot_page_ref, f_ref, wg_ref, wu_ref, wd_ref, q_ref, bias_ref = refs[:7]
        k_refs = refs[7:7 + side_pages]
        v_refs = refs[7 + side_pages:7 + 2 * side_pages]
        delta_ref, pacc_ref, ptot_ref = refs[7 + 2 * side_pages:]
    else:
        f_ref, wg_ref, wu_ref, wd_ref, delta_ref = refs
    tf = wd_ref.shape[0]
    i, j = pl.program_id(0), pl.program_id(1)

    @pl.when(j == 0)
    def _():
        delta_ref[...] = jnp.zeros_like(delta_ref)

    gate_up = jnp.dot(f_ref[...], jnp.concatenate([wg_ref[...], wu_ref[...]], axis=1),
                      preferred_element_type=F32)
    gate, up = gate_up[:, :tf], gate_up[:, tf:]
    if side_pages:
        first = (i * pl.num_programs(1) + j) * side_pages
        bias2 = bias_ref[...] * LOG2E
        weights = []
        for r in range(side_pages):
            seq = jnp.minimum(first + r, total_pages - 1) // pages_per_seq
            w, total = _page_weights(q_ref[seq].astype(BF16), bias2, k_refs[r])
            ptot_ref[r] = jnp.broadcast_to(total, ptot_ref.shape[1:])
            weights.append(w)
    act = (gate * jax.nn.sigmoid(gate) * up).astype(BF16)
    delta_ref[...] += jnp.dot(act, wd_ref[...], preferred_element_type=F32)
    if side_pages:
        for r in range(side_pages):
            pacc_ref[r] = _page_output(weights[r], v_refs[r])


def _ffn(f, wg_b, wu_b, wd_b, *, tm, tf, side=None, side_pages=0):
    T, d_model = f.shape
    d_ff = wd_b.shape[0]
    assert T % tm == 0 and d_ff % tf == 0
    grid = (T // tm, d_ff // tf)
    in_specs = [pl.BlockSpec((tm, d_model), lambda i, j, *_: (i, 0)),
                pl.BlockSpec((d_model, tf), lambda i, j, *_: (0, j)),
                pl.BlockSpec((d_model, tf), lambda i, j, *_: (0, j)),
                pl.BlockSpec((tf, d_model), lambda i, j, *_: (j, 0))]
    out_specs = [pl.BlockSpec((tm, d_model), lambda i, j, *_: (i, 0))]
    out_shape = [jax.ShapeDtypeStruct((T, d_model), F32)]
    args = [f, wg_b, wu_b, wd_b]
    if side is None:
        return pl.pallas_call(
            functools.partial(_ffn_kernel, side_pages=0, pages_per_seq=0, total_pages=0),
            grid=grid, in_specs=in_specs, out_specs=out_specs, out_shape=out_shape,
            compiler_params=_cparams("arbitrary", "arbitrary"), name="ffn",
        )(*args)[0]

    q, sb_bias, cache_k, cache_v, layer, page_table, k_new, v_new, new_key_visible = side
    n_seq, pages_per_seq = page_table.shape
    _, _, page, n_heads, hd = cache_k.shape
    n = page * n_heads
    total_pages = n_seq * pages_per_seq
    n_steps = grid[0] * grid[1]
    assert n_heads == SUBLANES and hd == HEAD_DIM and n & (n - 1) == 0
    assert n_steps * side_pages >= total_pages

    slot_page = jnp.pad(page_table.reshape(-1), (0, n_steps * side_pages - total_pages), mode="edge")

    def page_spec(r):
        return pl.BlockSpec((None, None, page, n_heads, hd),
                            lambda i, j, sp: (layer, sp[(i * grid[1] + j) * side_pages + r], 0, 0, 0))

    step_pages = lambda: pl.BlockSpec((side_pages, n_heads, hd), lambda i, j, pt: (i * grid[1] + j, 0, 0))
    part = jax.ShapeDtypeStruct((n_steps * side_pages, n_heads, hd), F32)
    delta, pacc, ptot = pl.pallas_call(
        functools.partial(_ffn_kernel, side_pages=side_pages, pages_per_seq=pages_per_seq,
                          total_pages=total_pages),
        grid_spec=pltpu.PrefetchScalarGridSpec(
            num_scalar_prefetch=1,
            grid=grid,
            in_specs=in_specs
                     + [pl.BlockSpec((n_seq, n_heads, hd), lambda i, j, pt: (0, 0, 0)),
                        pl.BlockSpec((n_heads, n), lambda i, j, pt: (0, 0))]
                     + [page_spec(r) for r in range(side_pages)] * 2,
            out_specs=out_specs + [step_pages(), step_pages()],
        ),
        out_shape=out_shape + [part, part],
        compiler_params=_cparams("arbitrary", "arbitrary"),
        name="ffn_with_decode_pages",
    )(slot_page, *args, q, jnp.broadcast_to(sb_bias[:, None], (n_heads, n)),
      *([cache_k] * side_pages), *([cache_v] * side_pages))
    return delta, _decode_combine(pacc, ptot, q, k_new, v_new, sb_bias, pages_per_seq, new_key_visible)


def _ple_kernel(h1_ref, delta_ref, p_ref, gple_ref, wg_ref, wp_ref, o_ref):
    h2 = h1_ref[...] + delta_ref[...]
    hn = (_rms_rows(h2) * gple_ref[...]).astype(BF16)
    gate = jax.nn.sigmoid(jnp.dot(hn, wg_ref[...], preferred_element_type=F32))
    proj = jnp.dot(p_ref[...].astype(BF16), wp_ref[...], preferred_element_type=F32)
    o_ref[...] = h2 + gate * proj


def _ple(h1, delta, p, g_ple, wpg_b, wpp_b, *, tm):
    T, d_model = h1.shape
    ple_dim = p.shape[1]
    assert T % tm == 0
    row_blk = lambda i: (i, 0)
    const2 = lambda i: (0, 0)
    return pl.pallas_call(
        _ple_kernel,
        grid=(T // tm,),
        in_specs=[pl.BlockSpec((tm, d_model), row_blk),
                  pl.BlockSpec((tm, d_model), row_blk),
                  pl.BlockSpec((tm, ple_dim), row_blk),
                  pl.BlockSpec((1, d_model), const2),
                  _resident((d_model, d_model)),
                  _resident((ple_dim, d_model))],
        out_specs=pl.BlockSpec((tm, d_model), row_blk),
        out_shape=jax.ShapeDtypeStruct((T, d_model), F32),
        compiler_params=_cparams("arbitrary"),
        name="ple",
    )(h1, delta, p, g_ple, wpg_b, wpp_b)


PROMPT_FFN_ROWS = 1024
FFN_COLS = 256


def kernel(x_prompt, x_sample, cache_k, cache_v, page_table, p_prompt, p_sample, g_mix, w_in, g_q, g_k, sb_bias, sgu_ln_g, sgu_ln_b, w_s, b_s, g_out_sb, g_out_cm, w_out, g_ffn, w_gate, w_up, w_down, g_ple, w_ple_gate, w_ple_proj):
    depth = w_in.shape[0]
    B, L, d_model = x_prompt.shape
    Bs, Ls, _ = x_sample.shape
    n_sb = cache_k.shape[3]
    n_cm = w_s.shape[1]
    d_ff = w_gate.shape[2]
    sbw, cmw = n_sb * HEAD_DIM, n_cm * HEAD_DIM
    assert Ls == 1 and n_sb == n_cm and cache_k.shape[4] == HEAD_DIM
    assert (page_table.shape[1] * cache_k.shape[2]) % CHUNK == 0 and L % CHUNK == 0
    ffn_steps = (B * L // PROMPT_FFN_ROWS) * (d_ff // FFN_COLS)
    side_pages = -(-page_table.size // ffn_steps)

    past_len = page_table.shape[1] * cache_k.shape[2]
    new_key_visible = past_len < past_len
    heads = lambda a: a.reshape(Bs, n_sb, HEAD_DIM)

    hp = x_prompt.reshape(B * L, d_model)
    hs = x_sample.reshape(Bs * Ls, d_model)
    row = lambda a: a.reshape(1, -1)
    kp_l, vp_l, ks_l, vs_l, cs_l = [], [], [], [], []
    for i in range(depth):
        w_in_b = w_in[i].astype(BF16)
        proj_args = (row(g_mix[i]), w_in_b, row(g_q[i]), row(g_k[i]), row(sgu_ln_g[i]), row(sgu_ln_b[i]),
                     w_s[i], b_s[i].T, row(g_out_cm[i]))

        q_b, k_f, k_b, v_f, v_b, ocm_n = _in_proj(hp, *proj_args, tm=512, fresh_chunk_rows=False)
        q_s, ks_f, _, vs_f, _, ocm_s, cs = _in_proj(hs, *proj_args, tm=Bs, fresh_chunk_rows=True)

        o_sb, (wg_b, wu_b, wd_b, w_out_b, wpg_b, wpp_b) = _sb_prompt(
            q_b.reshape(B, L, sbw), k_b.reshape(B, L, sbw), v_b.reshape(B, L, sbw), sb_bias[i],
            to_cast=(w_gate[i], w_up[i], w_down[i], w_out[i], w_ple_gate[i], w_ple_proj[i]),
            blk=256, heads_per_step=8)
        out_args = (row(g_out_sb[i]), w_out_b, row(g_ffn[i]))
        ple_args = (row(g_ple[i]), wpg_b, wpp_b)
        h1, f = _out_proj(o_sb.reshape(B * L, sbw), ocm_n, hp, *out_args, tm=512)
        delta, o_sb_s = _ffn(f, wg_b, wu_b, wd_b, tm=PROMPT_FFN_ROWS, tf=FFN_COLS, side_pages=side_pages,
                             side=(heads(q_s), sb_bias[i], cache_k, cache_v, i, page_table,
                                   heads(ks_f), heads(vs_f), new_key_visible))
        hp = _ple(h1, delta, p_prompt[i].reshape(B * L, -1), *ple_args, tm=512)
        kp_l.append(k_f.reshape(B, L, n_sb, HEAD_DIM))
        vp_l.append(v_f.reshape(B, L, n_sb, HEAD_DIM))

        h1_s, f_s = _out_proj(o_sb_s, ocm_s, hs, *out_args, tm=Bs)
        delta_s = _ffn(f_s, wg_b, wu_b, wd_b, tm=Bs, tf=2 * FFN_COLS)
        hs = _ple(h1_s, delta_s, p_sample[i].reshape(Bs, -1), *ple_args, tm=Bs)
        ks_l.append(ks_f.reshape(Bs, Ls, n_sb, HEAD_DIM))
        vs_l.append(vs_f.reshape(Bs, Ls, n_sb, HEAD_DIM))
        cs_l.append(cs.reshape(Bs, Ls, n_cm, HEAD_DIM))

    return (hp.reshape(B, L, d_model), hs.reshape(Bs, Ls, d_model),
            jnp.stack(kp_l), jnp.stack(vp_l), jnp.stack(ks_l), jnp.stack(vs_l), jnp.stack(cs_l))
```
